```python
import jax
import jax.numpy as jnp
from jax import lax
import numpy as np

D_MODEL = 1024
BATCH = 1
SEQ = 16384
DEPTH = 1

GRID_W = 64
CTX_LEN = 256
N_DIR = 2
N_MOD = 6
NORM_EPS = 1e-6

RWKV_HEAD = 64
RWKV_HEADS = 8
D_RWKV = RWKV_HEADS * RWKV_HEAD
LORA_W = 64
LORA_A = 64
LORA_G = 128
RWKV_COLS = 3 * D_RWKV + LORA_W + LORA_A + LORA_G
GN_EPS = 64e-5

MLSTM_HEADS = 4
MLSTM_DQK = 64
MLSTM_DV = 128
D_MLSTM_QK = MLSTM_HEADS * MLSTM_DQK
D_MLSTM = MLSTM_HEADS * MLSTM_DV
MLSTM_CHUNK = 64

IN_SIZES = (RWKV_COLS, 2 * D_MLSTM_QK, D_MLSTM, N_DIR * MLSTM_HEADS, N_DIR * MLSTM_HEADS, D_MLSTM, D_MODEL, D_MODEL)
IN_COLS = RWKV_COLS + 2 * D_MLSTM_QK + 2 * D_MLSTM + 2 * N_DIR * MLSTM_HEADS + 2 * D_MODEL

PEER_HEADS = 8
N_KEYS = 128
N_EXPERTS = N_KEYS * N_KEYS
PEER_TOPK = 16
PEER_QDIM = 256
KEY_DIM = PEER_QDIM // 2
PEER_BLOCK = 128

kernel_name = "hybrid_rwkv7_mlstm_peer_dit_layer"


def split_sizes(z, sizes):
    return jnp.split(z, np.cumsum(sizes)[:-1].tolist(), axis=-1)


def rmsnorm(z, g):
    zf = z.astype(jnp.float32)
    y = zf * lax.rsqrt(jnp.mean(zf * zf, axis=-1, keepdims=True) + NORM_EPS)
    return y.astype(z.dtype) * g


def modulate(h, shift, scale):
    return h * (1.0 + scale) + shift


def centred_conv3(z, w, axis):
    n = z.shape[axis]
    pad = [(0, 0)] * z.ndim
    pad[axis] = (1, 1)
    zp = jnp.pad(z, pad)
    prev = lax.slice_in_dim(zp, 0, n, axis=axis)
    nxt = lax.slice_in_dim(zp, 2, n + 2, axis=axis)
    return w[0] * prev + w[1] * z + w[2] * nxt


def both_dirs(z):
    return jnp.stack([z, jnp.flip(z, axis=1)], axis=2)


def orient(z):
    return jnp.stack([z[:, :, 0], jnp.flip(z[:, :, 1], axis=1)], axis=2)


def rwkv_branch(feats, S0, lp, need_out):
    B, T, _ = feats.shape
    r, k, v, wd, ad, gd = split_sizes(feats.astype(jnp.float32), (D_RWKV, D_RWKV, D_RWKV, LORA_W, LORA_A, LORA_G))
    w_log = -jax.nn.softplus(-(lp["rwkv_w0"] + jnp.einsum("btr,zrc->btzc", jnp.tanh(wd), lp["rwkv_w_up"]))) - 0.5
    decay = jnp.exp(-jnp.exp(w_log))
    a = jax.nn.sigmoid(lp["rwkv_a0"] + jnp.einsum("btr,zrc->btzc", ad, lp["rwkv_a_up"]))
    kk = (k * lp["rwkv_k_k"]).reshape(B, T, RWKV_HEADS, RWKV_HEAD)
    kk = (kk / jnp.maximum(jnp.linalg.norm(kk, axis=-1, keepdims=True), 1e-12)).reshape(B, T, D_RWKV)
    k_eff = k[:, :, None] * (1.0 + (a - 1.0) * lp["rwkv_k_a"])

    def heads(z):
        return jnp.moveaxis(z.reshape(B, T, N_DIR, RWKV_HEADS, RWKV_HEAD), 1, 0)

    xs = (heads(orient(k_eff)), heads(both_dirs(v)), heads(orient(decay)), heads(both_dirs(kk)), heads(orient(a)))
    if need_out:
        xs = xs + (heads(both_dirs(r)),)

    def step(S, inp):
        k_t, v_t, w_t, kk_t, a_t = inp[:5]
        s_kk = jnp.einsum("...vk,...k->...v", S, kk_t)
        S = S * w_t[..., None, :] - s_kk[..., :, None] * (kk_t * a_t)[..., None, :] + v_t[..., :, None] * k_t[..., None, :]
        if need_out:
            return S, jnp.einsum("...vk,...k->...v", S, inp[5])
        return S, None

    S, out = lax.scan(step, S0, xs)
    if not need_out:
        return None, S
    out = orient(jnp.moveaxis(out, 0, 1))
    mu = jnp.mean(out, axis=-1, keepdims=True)
    var = jnp.mean((out - mu) ** 2, axis=-1, keepdims=True)
    ln_w = lp["rwkv_ln_w"].reshape(RWKV_HEADS, RWKV_HEAD)
    ln_b = lp["rwkv_ln_b"].reshape(RWKV_HEADS, RWKV_HEAD)
    gn = (out - mu) * lax.rsqrt(var + GN_EPS) * ln_w + ln_b
    rh = r.reshape(B, T, 1, RWKV_HEADS, RWKV_HEAD)
    kh = k_eff.reshape(B, T, N_DIR, RWKV_HEADS, RWKV_HEAD)
    vh = v.reshape(B, T, 1, RWKV_HEADS, RWKV_HEAD)
    bonus = jnp.sum(rh * kh * lp["rwkv_r_k"].reshape(RWKV_HEADS, RWKV_HEAD), axis=-1, keepdims=True) * vh
    y = jnp.sum(gn + bonus, axis=2).reshape(B, T, D_RWKV)
    g = jax.nn.sigmoid(gd) @ lp["rwkv_g_up"]
    return (y * g).astype(feats.dtype), S


def mlstm_branch(qk, v, i_pre, f_pre, o_pre, state0, lp, need_out):
    B, T, _ = v.shape
    H = MLSTM_HEADS
    nc = T // MLSTM_CHUNK
    q, k = jnp.split(jax.nn.silu(qk.astype(jnp.float32)), 2, axis=-1)
    q = both_dirs(q.reshape(B, T, H, MLSTM_DQK))
    k = both_dirs(k.reshape(B, T, H, MLSTM_DQK) * MLSTM_DQK ** -0.5)
    vv = both_dirs(v.astype(jnp.float32).reshape(B, T, H, MLSTM_DV))
    ig = orient(i_pre.astype(jnp.float32).reshape(B, T, N_DIR, H) + lp["mlstm_i_b"])
    lf = orient(jax.nn.log_sigmoid(f_pre.astype(jnp.float32).reshape(B, T, N_DIR, H) + lp["mlstm_f_b"]))

    def chunks(z):
        z = z.reshape((B, nc, MLSTM_CHUNK) + z.shape[2:])
        return jnp.moveaxis(jnp.moveaxis(z, 1, 0), 2, 4)

    causal = jnp.tril(jnp.ones((MLSTM_CHUNK, MLSTM_CHUNK), dtype=bool))

    def step(carry, inp):
        C, n, m = carry
        kc, vc, ic, fc = inp[:4]
        b = jnp.cumsum(fc, axis=-1)
        b_end = b[..., -1]
        g_end = b_end[..., None] - b + ic
        m_new = jnp.maximum(b_end + m, jnp.max(g_end, axis=-1))
        w_end = jnp.exp(g_end - m_new[..., None])
        keep = jnp.exp(b_end + m - m_new)
        C_new = keep[..., None, None] * C + jnp.einsum("...sv,...sd->...vd", vc * w_end[..., None], kc)
        n_new = keep[..., None] * n + jnp.einsum("...s,...sd->...d", w_end, kc)
        if not need_out:
            return (C_new, n_new, m_new), None
        qc = inp[4]
        log_d = jnp.where(causal, b[..., :, None] - b[..., None, :] + ic[..., None, :], -jnp.inf)
        inter = b + m[..., None]
        m_row = jnp.maximum(inter, jnp.max(log_d, axis=-1))
        s = jnp.einsum("...jd,...sd->...js", qc, kc) * jnp.exp(log_d - m_row[..., None])
        a_inter = jnp.exp(inter - m_row)
        num = jnp.einsum("...js,...sv->...jv", s, vc) + a_inter[..., None] * jnp.einsum("...vd,...jd->...jv", C, qc)
        den = jnp.sum(s, axis=-1) + a_inter * jnp.einsum("...d,...jd->...j", n, qc)
        h = num / jnp.maximum(jnp.abs(den), jnp.exp(-m_row))[..., None]
        return (C_new, n_new, m_new), h

    xs = (chunks(k), chunks(vv), chunks(ig), chunks(lf))
    if need_out:
        xs = xs + (chunks(q),)
    state, h = lax.scan(step, state0, xs)
    if not need_out:
        return None, state
    h = jnp.moveaxis(jnp.moveaxis(h, 4, 2), 0, 1).reshape(B, T, N_DIR, H, MLSTM_DV)
    h = orient(h)
    h = h * lax.rsqrt(jnp.mean(h * h, axis=-1, keepdims=True) + NORM_EPS) * lp["mlstm_norm_g"].reshape(H, MLSTM_DV)
    y = jnp.sum(h, axis=2).reshape(B, T, D_MLSTM) * jax.nn.sigmoid(o_pre.astype(jnp.float32))
    return y.astype(v.dtype), state


def token_mixer(P, rwkv_s0, mlstm_s0, lp, on_grid, need_out):
    B, T, _ = P.shape
    rw, qk, mv, ml_i, ml_f, ml_o, gate_a, gate_b = split_sizes(P, IN_SIZES)
    if on_grid:
        rows = T // GRID_W
        rw = centred_conv3(rw.reshape(B, rows, GRID_W, -1), lp["rwkv_conv"], 2).reshape(B, T, -1)

        def to_col(z):
            return z.reshape(B, rows, GRID_W, -1).transpose(0, 2, 1, 3)

        qk = centred_conv3(to_col(qk), lp["mlstm_conv"], 2).reshape(B, T, -1)
        mv, ml_i, ml_f, ml_o = [to_col(z).reshape(B, T, -1) for z in (mv, ml_i, ml_f, ml_o)]
    else:
        rw = centred_conv3(rw, lp["rwkv_conv"], 1)
        qk = centred_conv3(qk, lp["mlstm_conv"], 1)
    y_a, rwkv_s = rwkv_branch(rw, rwkv_s0, lp, need_out)
    y_b, mlstm_s = mlstm_branch(qk, mv, ml_i, ml_f, ml_o, mlstm_s0, lp, need_out)
    if not need_out:
        return None, rwkv_s, mlstm_s
    if on_grid:
        y_b = y_b.reshape(B, GRID_W, rows, -1).transpose(0, 2, 1, 3).reshape(B, T, -1)
    merged = jax.nn.sigmoid(gate_a) * (y_a @ lp["w_branch_a"]) + jax.nn.sigmoid(gate_b) * (y_b @ lp["w_branch_b"])
    return merged @ lp["w_out"], rwkv_s, mlstm_s


def peer_ffn(h, wq, keys, u, v):
    B, T, D = h.shape
    nb = (B * T) // PEER_BLOCK

    def block(hb):
        q = (hb @ wq).reshape(PEER_BLOCK, PEER_HEADS, 2, KEY_DIM)
        s = jnp.einsum("mhpd,hpkd->mhpk", q, keys).astype(jnp.float32)
        s1, i1 = lax.top_k(s[:, :, 0], PEER_TOPK)
        s2, i2 = lax.top_k(s[:, :, 1], PEER_TOPK)
        cand = (s1[..., :, None] + s2[..., None, :]).reshape(PEER_BLOCK, PEER_HEADS, PEER_TOPK * PEER_TOPK)
        sc, ci = lax.top_k(cand, PEER_TOPK)
        idx = (jnp.take_along_axis(i1, ci // PEER_TOPK, axis=-1) * N_KEYS
               + jnp.take_along_axis(i2, ci % PEER_TOPK, axis=-1)).reshape(PEER_BLOCK, -1)
        g = jax.nn.softmax(sc, axis=-1).reshape(PEER_BLOCK, -1)
        act = jax.nn.gelu(jnp.einsum("med,md->me", u[idx], hb), approximate=False)
        return jnp.einsum("me,med->md", (g * act).astype(hb.dtype), v[idx])

    return lax.map(block, h.reshape(nb, PEER_BLOCK, D)).reshape(B, T, D)


def setup_inputs(seed: int = 0) -> dict:
    key = jax.random.key(seed)
    ks = jax.random.split(key, 32)
    L = DEPTH

    def nrm(k, shape, s):
        return jax.random.normal(k, shape, jnp.float32) * s

    shift_init = jnp.array([0.25, 0.5, 0.25], jnp.float32)[None, :, None]
    return {
        "x": nrm(ks[0], (BATCH, SEQ, D_MODEL), 1.0),
        "c": nrm(ks[1], (BATCH, D_MODEL), 1.0),
        "ctx": nrm(ks[2], (BATCH, CTX_LEN, D_MODEL), 1.0),
        "c_ctx": nrm(ks[3], (D_MODEL,), 1.0),
        "ada_w": nrm(ks[4], (L, D_MODEL, N_MOD * D_MODEL), 0.3 * D_MODEL ** -0.5),
        "ada_b": nrm(ks[5], (L, N_MOD * D_MODEL), 0.02),
        "norm1_g": 1.0 + nrm(ks[6], (L, D_MODEL), 0.1),
        "w_in": nrm(ks[7], (L, D_MODEL, IN_COLS), D_MODEL ** -0.5),
        "rwkv_conv": shift_init + nrm(ks[8], (L, 3, RWKV_COLS), 0.05),
        "rwkv_w0": jax.random.uniform(ks[9], (L, N_DIR, D_RWKV), jnp.float32, minval=-6.0, maxval=1.0),
        "rwkv_w_up": nrm(ks[10], (L, N_DIR, LORA_W, D_RWKV), 0.5 * LORA_W ** -0.5),
        "rwkv_a0": nrm(ks[11], (L, N_DIR, D_RWKV), 0.5),
        "rwkv_a_up": nrm(ks[12], (L, N_DIR, LORA_A, D_RWKV), 0.5 * LORA_A ** -0.5),
        "rwkv_g_up": nrm(ks[13], (L, LORA_G, D_RWKV), LORA_G ** -0.5),
        "rwkv_k_k": 0.85 + nrm(ks[14], (L, D_RWKV), 0.05),
        "rwkv_k_a": 1.0 + nrm(ks[15], (L, D_RWKV), 0.05),
        "rwkv_r_k": nrm(ks[16], (L, D_RWKV), 0.1),
        "rwkv_ln_w": 1.0 + nrm(ks[17], (L, D_RWKV), 0.1),
        "rwkv_ln_b": nrm(ks[18], (L, D_RWKV), 0.01),
        "mlstm_conv": shift_init + nrm(ks[19], (L, 3, 2 * D_MLSTM_QK), 0.05),
        "mlstm_i_b": nrm(ks[20], (L, N_DIR, MLSTM_HEADS), 0.5),
        "mlstm_f_b": 3.0 + nrm(ks[21], (L, N_DIR, MLSTM_HEADS), 0.5),
        "mlstm_norm_g": 1.0 + nrm(ks[22], (L, D_MLSTM), 0.1),
        "w_branch_a": nrm(ks[23], (L, D_RWKV, D_MODEL), D_RWKV ** -0.5),
        "w_branch_b": nrm(ks[24], (L, D_MLSTM, D_MODEL), D_MLSTM ** -0.5),
        "w_out": nrm(ks[25], (L, D_MODEL, D_MODEL), D_MODEL ** -0.5),
        "norm2_g": 1.0 + nrm(ks[26], (L, D_MODEL), 0.1),
        "peer_wq": nrm(ks[27], (L, D_MODEL, PEER_HEADS * PEER_QDIM), D_MODEL ** -0.5),
        "peer_keys": nrm(ks[28], (L, PEER_HEADS, 2, N_KEYS, KEY_DIM), KEY_DIM ** -0.5),
        "peer_u": nrm(ks[29], (L, N_EXPERTS, D_MODEL), D_MODEL ** -0.5),
        "peer_v": nrm(ks[30], (L, N_EXPERTS, D_MODEL), PEER_HEADS ** -0.5),
        "final_g": 1.0 + nrm(ks[31], (D_MODEL,), 0.1),
    }


def reference(x, c, ctx, c_ctx, ada_w, ada_b, norm1_g, w_in, rwkv_conv, rwkv_w0, rwkv_w_up, rwkv_a0,
              rwkv_a_up, rwkv_g_up, rwkv_k_k, rwkv_k_a, rwkv_r_k, rwkv_ln_w, rwkv_ln_b, mlstm_conv,
              mlstm_i_b, mlstm_f_b, mlstm_norm_g, w_branch_a, w_branch_b, w_out, norm2_g, peer_wq,
              peer_keys, peer_u, peer_v, final_g):
    B = x.shape[0]
    silu_c = jax.nn.silu(c)
    silu_cc = jax.nn.silu(c_ctx)
    for i in range(DEPTH):
        last = i == DEPTH - 1
        lp = {
            "rwkv_conv": rwkv_conv[i], "rwkv_w0": rwkv_w0[i], "rwkv_w_up": rwkv_w_up[i],
            "rwkv_a0": rwkv_a0[i], "rwkv_a_up": rwkv_a_up[i], "rwkv_g_up": rwkv_g_up[i],
            "rwkv_k_k": rwkv_k_k[i], "rwkv_k_a": rwkv_k_a[i], "rwkv_r_k": rwkv_r_k[i],
            "rwkv_ln_w": rwkv_ln_w[i], "rwkv_ln_b": rwkv_ln_b[i], "mlstm_conv": mlstm_conv[i],
            "mlstm_i_b": mlstm_i_b[i], "mlstm_f_b": mlstm_f_b[i], "mlstm_norm_g": mlstm_norm_g[i],
            "w_branch_a": w_branch_a[i], "w_branch_b": w_branch_b[i], "w_out": w_out[i],
        }
        sh1, sc1, gt1, sh2, sc2, gt2 = [z[:, None] for z in jnp.split(silu_c @ ada_w[i] + ada_b[i], N_MOD, axis=-1)]
        csh1, csc1, cgt1, csh2, csc2, cgt2 = jnp.split(silu_cc @ ada_w[i] + ada_b[i], N_MOD, axis=-1)

        rwkv0 = jnp.zeros((B, N_DIR, RWKV_HEADS, RWKV_HEAD, RWKV_HEAD), jnp.float32)
        mlstm0 = (jnp.zeros((B, N_DIR, MLSTM_HEADS, MLSTM_DV, MLSTM_DQK), jnp.float32),
                  jnp.zeros((B, N_DIR, MLSTM_HEADS, MLSTM_DQK), jnp.float32),
                  jnp.zeros((B, N_DIR, MLSTM_HEADS), jnp.float32))
        hc = modulate(rmsnorm(ctx, norm1_g[i]), csh1, csc1)
        out_c, rwkv_s, mlstm_s = token_mixer(hc @ w_in[i], rwkv0, mlstm0, lp, False, not last)

        hx = modulate(rmsnorm(x, norm1_g[i]), sh1, sc1)
        out_x, _, _ = token_mixer(hx @ w_in[i], rwkv_s, mlstm_s, lp, True, True)
        x = x + gt1 * out_x
        x = x + gt2 * peer_ffn(modulate(rmsnorm(x, norm2_g[i]), sh2, sc2), peer_wq[i], peer_keys[i], peer_u[i], peer_v[i])

        if not last:
            ctx = ctx + cgt1 * out_c
            ctx = ctx + cgt2 * peer_ffn(modulate(rmsnorm(ctx, norm2_g[i]), csh2, csc2), peer_wq[i], peer_keys[i], peer_u[i], peer_v[i])
    return rmsnorm(x, final_g)
```

```python
import functools

import jax
import jax.numpy as jnp
from jax import lax
from jax.experimental import pallas as pl
from jax.experimental.pallas import tpu as pltpu

F32 = jnp.float32
BF16 = jnp.bfloat16

D_MODEL = 1024
GRID_W = 64
N_MOD = 6
NORM_EPS = 1e-6

RWKV_HEAD = 64
RWKV_HEADS = 8
D_RWKV = RWKV_HEADS * RWKV_HEAD
LORA_W = 64
LORA_A = 64
LORA_G = 128
RWKV_COLS = 3 * D_RWKV + LORA_W + LORA_A + LORA_G
GN_EPS = 64e-5
RWKV_PAIRS = RWKV_HEADS // 2

MLSTM_HEADS = 4
MLSTM_DQK = 64
MLSTM_DV = 128
D_MLSTM_QK = MLSTM_HEADS * MLSTM_DQK
D_MLSTM = MLSTM_HEADS * MLSTM_DV
N_DIR = 2
CHUNK = 64

PEER_HEADS = 8
N_KEYS = 128
PEER_TOPK = 16
KEY_DIM = 128
PEER_QDIM = 2 * KEY_DIM

LANES = 128
VMEM_LIMIT = 56 * 1024 * 1024

_HI = lax.Precision.HIGHEST


def _dot(a, b, precision=None):
    return jnp.dot(a, b, preferred_element_type=F32, precision=precision)


def _dot_nt(a, b):
    return lax.dot_general(a, b, (((1,), (1,)), ((), ())), preferred_element_type=F32)


def _dot_tn(a, b):
    return lax.dot_general(a, b, (((0,), (0,)), ((), ())), preferred_element_type=F32)


def _sigmoid(x):
    return 1.0 / (1.0 + jnp.exp(-x))


def _params(sem):
    return pltpu.CompilerParams(dimension_semantics=sem, vmem_limit_bytes=VMEM_LIMIT)


def _scan_rows(x, reverse):
    n = x.shape[0]
    row = lax.broadcasted_iota(jnp.int32, x.shape, 0)
    d = 1
    while d < n:
        if reverse:
            x = x + jnp.where(row < n - d, pltpu.roll(x, n - d, axis=0), 0.0)
        else:
            x = x + jnp.where(row >= d, pltpu.roll(x, d, axis=0), 0.0)
        d *= 2
    return x


def _conv3_rows(x, w, period):
    n = x.shape[0]
    pos = lax.broadcasted_iota(jnp.int32, x.shape, 0) % period
    prev = jnp.where(pos == 0, 0.0, pltpu.roll(x, 1, axis=0))
    nxt = jnp.where(pos == period - 1, 0.0, pltpu.roll(x, n - 1, axis=0))
    return w[0:1] * prev + w[1:2] * x + w[2:3] * nxt


def _ada_kernel(c_ref, w_ref, b_ref, o_ref):
    c = c_ref[...]
    s = c * _sigmoid(c)
    o_ref[...] = _dot(s, w_ref[...], precision=_HI) + b_ref[...]


def _ada(cc, w, b):
    n = w.shape[1]
    bn = n // 4
    return pl.pallas_call(
        _ada_kernel,
        grid=(n // bn,),
        in_specs=[pl.BlockSpec((8, D_MODEL), lambda i: (0, 0)),
                  pl.BlockSpec((D_MODEL, bn), lambda i: (0, i)),
                  pl.BlockSpec((1, bn), lambda i: (0, i))],
        out_specs=pl.BlockSpec((8, bn), lambda i: (0, i)),
        out_shape=jax.ShapeDtypeStruct((8, n), F32),
        compiler_params=_params(("arbitrary",)),
        name="ada",
    )(cc, w, b)


def _proj_kernel(nw, x_ref, g_ref, sh_ref, sc_ref, *refs):
    x = x_ref[...]
    y = x * lax.rsqrt(jnp.mean(x * x, axis=-1, keepdims=True) + NORM_EPS)
    h = (y * g_ref[...]) * (1.0 + sc_ref[...]) + sh_ref[...]
    hb = h.astype(BF16)
    for w_ref, o_ref in zip(refs[:nw], refs[nw:]):
        o_ref[...] = _dot(hb, w_ref[...])


def _proj(x, g, shift, scale, weights, tm):
    t = x.shape[0]
    row = lambda i: (i, 0)
    fix = lambda i: (0, 0)
    in_specs = [pl.BlockSpec((tm, D_MODEL), row)] + [pl.BlockSpec((1, D_MODEL), fix)] * 3
    in_specs += [pl.BlockSpec(w.shape, fix) for w in weights]
    return pl.pallas_call(
        functools.partial(_proj_kernel, len(weights)),
        grid=(t // tm,),
        in_specs=in_specs,
        out_specs=[pl.BlockSpec((tm, w.shape[1]), row) for w in weights],
        out_shape=[jax.ShapeDtypeStruct((t, w.shape[1]), F32) for w in weights],
        compiler_params=_params(("parallel",)),
        name="proj",
    )(x, g, shift, scale, *weights)


def _rwkv_prep(z, f_ref, conv_ref, w0_ref, wup_ref, a0_ref, aup_ref, kk_ref, ka_ref, gup_ref, period, dst):
    f = _conv3_rows(f_ref[...], conv_ref[...], period)
    r = f[:, 0:D_RWKV]
    k = f[:, D_RWKV:2 * D_RWKV]
    v = f[:, 2 * D_RWKV:3 * D_RWKV]
    o = 3 * D_RWKV
    wd = f[:, o:o + LORA_W]
    ad = f[:, o + LORA_W:o + LORA_W + LORA_A]
    gd = f[:, o + LORA_W + LORA_A:o + LORA_W + LORA_A + LORA_G]
    lw = w0_ref[z:z + 1, :] + _dot(jnp.tanh(wd).astype(BF16), wup_ref[z])
    w_log = -(jnp.maximum(-lw, 0.0) + jnp.log(1.0 + jnp.exp(-jnp.abs(lw)))) - 0.5
    a = _sigmoid(a0_ref[z:z + 1, :] + _dot(ad.astype(BF16), aup_ref[z]))
    g = _dot(_sigmoid(gd).astype(BF16), gup_ref[...])
    logw_s, kraw_s, a_s, keff_s, v_s, r_s, g_s = dst
    logw_s[z] = -jnp.exp(w_log)
    kraw_s[z] = k * kk_ref[...]
    a_s[z] = a
    keff_s[z] = k * (1.0 + (a - 1.0) * ka_ref[...])
    v_s[z] = v
    r_s[z] = r
    g_s[z] = g


def _rwkv_unit(z, lw, kraw, a, keff, v, r, rk, lnw, lnb, s_prev, masks):
    lane_lo, own, strict, incl, diag16, eye = masks[z]
    rev = z == 1

    def st(x):
        return jnp.concatenate([jnp.where(lane_lo, x, 0.0), jnp.where(lane_lo, 0.0, x)], axis=0)

    cum = _scan_rows(lw, rev)
    tot = cum[0:1, :] if rev else cum[CHUNK - 1:CHUNK, :]
    e_in = jnp.exp(cum)
    e_ex = jnp.exp(cum - lw)
    e_ng = jnp.exp(-cum)
    e_rem = jnp.exp(tot - cum)

    kr = st(kraw)
    inv = 1.0 / jnp.maximum(jnp.sqrt(jnp.sum(kr * kr, axis=1, keepdims=True)), 1e-12)
    ka = kraw * a
    xk = st(kraw * e_ex) * inv
    xr = st(r * e_in)
    yk = st(keff * e_ng)
    yb = st(ka * e_ng) * inv
    ykg = st(keff * e_rem)
    ybg = st(ka * e_rem) * inv
    vs = st(v)

    m = _dot_nt(jnp.concatenate([xk, xr], axis=0).astype(BF16), jnp.concatenate([yk, yb], axis=0).astype(BF16))
    akk = jnp.where(strict, m[:LANES, :LANES], 0.0)
    amat = m[:LANES, LANES:]
    ark = jnp.where(incl, m[LANES:, :LANES], 0.0)
    arb = jnp.where(incl, m[LANES:, LANES:], 0.0)
    a_d = jnp.where(strict & diag16, amat, 0.0)
    a_off = jnp.where(strict & jnp.logical_not(diag16), amat, 0.0)

    def mm(p, q):
        return _dot(p.astype(BF16), q.astype(BF16))

    a2 = mm(a_d, a_d)
    a4 = mm(a2, a2)
    a8 = mm(a4, a4)
    t_d = eye - a_d
    t_d = t_d + mm(t_d, a2)
    t_d = t_d + mm(t_d, a4)
    t_d = t_d + mm(t_d, a8)
    n1 = mm(t_d, a_off)
    n2 = mm(n1, n1)
    n3 = mm(n1, n2)
    tmat = mm(eye - n1 + n2 - n3, t_d)

    sb = s_prev.astype(BF16)
    vb = vs.astype(BF16)
    rhs = _dot_nt(xk.astype(BF16), sb) + _dot(akk.astype(BF16), vb)
    u = mm(tmat, rhs)
    ub = u.astype(BF16)
    o = _dot_nt(xr.astype(BF16), sb) + _dot(ark.astype(BF16), vb) - _dot(arb.astype(BF16), ub)
    s_new = s_prev * jnp.exp(tot) + _dot_tn(vb, ykg.astype(BF16)) - _dot_tn(ub, ybg.astype(BF16))

    mu = jnp.sum(o, axis=1, keepdims=True) * (1.0 / RWKV_HEAD)
    cen = jnp.where(own, o - mu, 0.0)
    var = jnp.sum(cen * cen, axis=1, keepdims=True) * (1.0 / RWKV_HEAD)
    gn = cen * lax.rsqrt(var + GN_EPS) * lnw + jnp.where(own, lnb, 0.0)
    bonus = jnp.sum(st(r * keff * rk), axis=1, keepdims=True) * vs
    y = gn + bonus
    return y[:CHUNK] + y[CHUNK:], s_new


def _rwkv_masks():
    i = lax.broadcasted_iota(jnp.int32, (LANES, LANES), 0)
    j = lax.broadcasted_iota(jnp.int32, (LANES, LANES), 1)
    lane_lo = lax.broadcasted_iota(jnp.int32, (CHUNK, LANES), 1) < RWKV_HEAD
    same = (i // CHUNK) == (j // CHUNK)
    eye = jnp.where(i == j, 1.0, 0.0).astype(F32)
    diag16 = (i // 16) == (j // 16)
    out = []
    for z in range(N_DIR):
        before = (j % CHUNK) > (i % CHUNK) if z == 1 else (j % CHUNK) < (i % CHUNK)
        strict = same & before
        incl = same & (before | (i == j))
        out.append((lane_lo, same, strict, incl, diag16, eye))
    return out


def _rwkv_kernel(period, nch, f0_ref, f1_ref, conv_ref, w0_ref, wup_ref, a0_ref, aup_ref, kk_ref, ka_ref, rk_ref,
                 lnw_ref, lnb_ref, gup_ref, sin_ref, y0_ref, y1_ref, sout_ref,
                 s_ref, logw_s, kraw_s, a_s, keff_s, v_s, r_s, g_s):
    step = pl.program_id(0)

    @pl.when(step == 0)
    def _():
        s_ref[...] = sin_ref[...]

    dst = (logw_s, kraw_s, a_s, keff_s, v_s, r_s, g_s)
    for z, f_ref in ((0, f0_ref), (1, f1_ref)):
        _rwkv_prep(z, f_ref, conv_ref, w0_ref, wup_ref, a0_ref, aup_ref, kk_ref, ka_ref, gup_ref, period, dst)

    masks = _rwkv_masks()
    y_refs = (y0_ref, y1_ref)

    def chunk_body(ci, carry):
        for z in range(N_DIR):
            r0 = pl.multiple_of((nch - 1 - ci if z == 1 else ci) * CHUNK, CHUNK)
            rows = pl.ds(r0, CHUNK)
            for p in range(RWKV_PAIRS):
                ls = slice(p * LANES, (p + 1) * LANES)
                y, s_new = _rwkv_unit(
                    z, logw_s[z, rows, ls], kraw_s[z, rows, ls], a_s[z, rows, ls], keff_s[z, rows, ls],
                    v_s[z, rows, ls], r_s[z, rows, ls], rk_ref[:, ls], lnw_ref[:, ls], lnb_ref[:, ls],
                    s_ref[z, p], masks)
                s_ref[z, p] = s_new
                y_refs[z][rows, ls] = y * g_s[z, rows, ls]
        return carry

    lax.fori_loop(0, nch, chunk_body, 0)

    @pl.when(step == pl.num_programs(0) - 1)
    def _():
        sout_ref[...] = s_ref[...]


def _rwkv(feat, prm, s_init, tb, period):
    t = feat.shape[0]
    nb = t // tb
    nch = tb // CHUNK
    fix2 = lambda i: (0, 0)
    fix3 = lambda i: (0, 0, 0)
    fix4 = lambda i: (0, 0, 0, 0)
    fwd = lambda i: (i, 0)
    bwd = lambda i: (nb - 1 - i, 0)
    vec = pl.BlockSpec((1, D_RWKV), fix2)
    state = pl.BlockSpec((N_DIR, RWKV_PAIRS, LANES, LANES), fix4)
    big = pltpu.VMEM((N_DIR, tb, D_RWKV), F32)
    return pl.pallas_call(
        functools.partial(_rwkv_kernel, period, nch),
        grid=(nb,),
        in_specs=[pl.BlockSpec((tb, RWKV_COLS), fwd), pl.BlockSpec((tb, RWKV_COLS), bwd),
                  pl.BlockSpec((3, RWKV_COLS), fix2),
                  pl.BlockSpec((N_DIR, D_RWKV), fix2), pl.BlockSpec((N_DIR, LORA_W, D_RWKV), fix3),
                  pl.BlockSpec((N_DIR, D_RWKV), fix2), pl.BlockSpec((N_DIR, LORA_A, D_RWKV), fix3),
                  vec, vec, vec, vec, vec,
                  pl.BlockSpec((LORA_G, D_RWKV), fix2), state],
        out_specs=[pl.BlockSpec((tb, D_RWKV), fwd), pl.BlockSpec((tb, D_RWKV), bwd), state],
        out_shape=[jax.ShapeDtypeStruct((t, D_RWKV), F32), jax.ShapeDtypeStruct((t, D_RWKV), F32),
                   jax.ShapeDtypeStruct((N_DIR, RWKV_PAIRS, LANES, LANES), F32)],
        scratch_shapes=[pltpu.VMEM((N_DIR, RWKV_PAIRS, LANES, LANES), F32)] + [big] * 7,
        compiler_params=_params(("arbitrary",)),
        name="rwkv",
    )(feat, feat, prm["conv"], prm["w0"], prm["w_up"], prm["a0"], prm["a_up"], prm["k_k"], prm["k_a"], prm["r_k"],
      prm["ln_w"], prm["ln_b"], prm["g_up"], s_init)


def _mlstm_kernel(nch, qk0_ref, qk1_ref, mv0_ref, mv1_ref, if0_ref, if1_ref, conv_ref, bias_ref, ng_ref,
                  cin_ref, nin_ref, min_ref, h0_ref, h1_ref, cout_ref, nout_ref, mout_ref,
                  c_s, n_s, m_s, qk_s, gi_s, lf_s):
    step = pl.program_id(0)
    rows_n = qk0_ref.shape[0]

    @pl.when(step == 0)
    def _():
        c_s[...] = cin_ref[...]
        n_s[...] = nin_ref[...]
        m_s[...] = min_ref[...]

    for z, (qk_ref, if_ref) in enumerate(((qk0_ref, if0_ref), (qk1_ref, if1_ref))):
        qk = _conv3_rows(qk_ref[...], conv_ref[...], rows_n)
        qk_s[z] = qk * _sigmoid(qk)
        gate = if_ref[...] + bias_ref[...]
        gi_s[z] = gate
        lf_s[z] = jnp.minimum(gate, 0.0) - jnp.log(1.0 + jnp.exp(-jnp.abs(gate)))

    ti = lax.broadcasted_iota(jnp.int32, (CHUNK, CHUNK), 0)
    si = lax.broadcasted_iota(jnp.int32, (CHUNK, CHUNK), 1)
    causal = (si <= ti, si >= ti)
    mv_refs = (mv0_ref, mv1_ref)
    h_refs = (h0_ref, h1_ref)
    nh = MLSTM_HEADS

    def chunk_body(ci, carry):
        for z in range(N_DIR):
            rev = z == 1
            r0 = pl.multiple_of((nch - 1 - ci if rev else ci) * CHUNK, CHUNK)
            rows = pl.ds(r0, CHUNK)
            gi = gi_s[z, rows, :]
            bcum = _scan_rows(lf_s[z, rows, :], rev)
            gi_t = gi.T
            bcum_t = bcum.T
            for h in range(nh):
                u = z * nh + h
                icol = gi[:, u:u + 1]
                bcol = bcum[:, 2 * nh + u:2 * nh + u + 1]
                irow = gi_t[u:u + 1, :]
                brow = bcum_t[2 * nh + u:2 * nh + u + 1, :]
                bend = bcol[0:1, :] if rev else bcol[CHUNK - 1:CHUNK, :]
                q = qk_s[z, rows, h * MLSTM_DQK:(h + 1) * MLSTM_DQK]
                k = qk_s[z, rows, D_MLSTM_QK + h * MLSTM_DQK:D_MLSTM_QK + (h + 1) * MLSTM_DQK] * (MLSTM_DQK ** -0.5)
                v = mv_refs[z][rows, h * MLSTM_DV:(h + 1) * MLSTM_DV]
                ct = c_s[u]
                nrow = n_s[u:u + 1, :]
                mprev = m_s[u:u + 1, 0:1]
                qb = q.astype(BF16)
                kb = k.astype(BF16)

                log_d = jnp.where(causal[z], bcol + (irow - brow), -jnp.inf)
                inter = bcol + mprev
                m_row = jnp.maximum(inter, jnp.max(log_d, axis=1, keepdims=True))
                smat = _dot_nt(qb, kb) * jnp.exp(log_d - m_row)
                a_int = jnp.exp(inter - m_row)
                num = _dot(smat.astype(BF16), v.astype(BF16)) + a_int * _dot(qb, ct.astype(BF16))
                den = jnp.sum(smat, axis=1, keepdims=True) + a_int * jnp.sum(q * nrow, axis=1, keepdims=True)
                hh = num / jnp.maximum(jnp.abs(den), jnp.exp(-m_row))
                hn = hh * lax.rsqrt(jnp.mean(hh * hh, axis=1, keepdims=True) + NORM_EPS)
                h_refs[z][rows, h * MLSTM_DV:(h + 1) * MLSTM_DV] = hn * ng_ref[:, h * MLSTM_DV:(h + 1) * MLSTM_DV]

                g_end = bend - bcol + icol
                m_new = jnp.maximum(bend + mprev, jnp.max(g_end, axis=0, keepdims=True))
                w_end = jnp.exp(g_end - m_new)
                keep = jnp.exp(bend + mprev - m_new)
                c_s[u] = keep * ct + _dot_tn(kb, (v * w_end).astype(BF16))
                n_s[u:u + 1, :] = keep * nrow + jnp.sum(k * w_end, axis=0, keepdims=True)
                m_s[u:u + 1, :] = jnp.broadcast_to(m_new, (1, LANES))
        return carry

    lax.fori_loop(0, nch, chunk_body, 0)

    @pl.when(step == pl.num_programs(0) - 1)
    def _():
        cout_ref[...] = c_s[...]
        nout_ref[...] = n_s[...]
        mout_ref[...] = m_s[...]


def _mlstm(qk, mv, gates, prm, state, rows, ncol):
    nch = rows // CHUNK
    nu = N_DIR * MLSTM_HEADS
    qk2 = qk.reshape(rows, ncol * 2 * D_MLSTM_QK)
    mv2 = mv.reshape(rows, ncol * D_MLSTM)
    g2 = gates.reshape(rows, ncol * LANES)
    fwd = lambda i: (0, i)
    bwd = lambda i: (0, ncol - 1 - i)
    fix2 = lambda i: (0, 0)
    fix3 = lambda i: (0, 0, 0)
    cspec = pl.BlockSpec((nu, MLSTM_DQK, MLSTM_DV), fix3)
    nspec = pl.BlockSpec((nu, MLSTM_DQK), fix2)
    mspec = pl.BlockSpec((nu, LANES), fix2)
    wide = lambda m: pl.BlockSpec((rows, 2 * D_MLSTM_QK), m)
    outs = pl.pallas_call(
        functools.partial(_mlstm_kernel, nch),
        grid=(ncol,),
        in_specs=[wide(fwd), wide(bwd), wide(fwd), wide(bwd),
                  pl.BlockSpec((rows, LANES), fwd), pl.BlockSpec((rows, LANES), bwd),
                  pl.BlockSpec((3, 2 * D_MLSTM_QK), fix2), pl.BlockSpec((1, LANES), fix2),
                  pl.BlockSpec((1, D_MLSTM), fix2), cspec, nspec, mspec],
        out_specs=[wide(fwd), wide(bwd), cspec, nspec, mspec],
        out_shape=[jax.ShapeDtypeStruct((rows, ncol * D_MLSTM), F32)] * 2
        + [jax.ShapeDtypeStruct((nu, MLSTM_DQK, MLSTM_DV), F32), jax.ShapeDtypeStruct((nu, MLSTM_DQK), F32),
           jax.ShapeDtypeStruct((nu, LANES), F32)],
        scratch_shapes=[pltpu.VMEM((nu, MLSTM_DQK, MLSTM_DV), F32), pltpu.VMEM((nu, MLSTM_DQK), F32),
                        pltpu.VMEM((nu, LANES), F32), pltpu.VMEM((N_DIR, rows, 2 * D_MLSTM_QK), F32),
                        pltpu.VMEM((N_DIR, rows, LANES), F32), pltpu.VMEM((N_DIR, rows, LANES), F32)],
        compiler_params=_params(("arbitrary",)),
        name="mlstm",
    )(qk2, qk2, mv2, mv2, g2, g2, prm["conv"], prm["bias"], prm["norm_g"], *state)
    h0, h1 = outs[0].reshape(rows * ncol, D_MLSTM), outs[1].reshape(rows * ncol, D_MLSTM)
    return h0, h1, tuple(outs[2:])


def _merge_kernel(x_ref, ya0_ref, ya1_ref, hb0_ref, hb1_ref, po_ref, ga_ref, gb_ref, wa_ref, wb_ref, wo_ref,
                  gt1_ref, g2_ref, sh2_ref, sc2_ref, wq_ref, x1_ref, h2_ref, q_ref):
    ya = ya0_ref[...] + ya1_ref[...]
    yb = (hb0_ref[...] + hb1_ref[...]) * _sigmoid(po_ref[...])
    merged = (_sigmoid(ga_ref[...]) * _dot(ya.astype(BF16), wa_ref[...])
              + _sigmoid(gb_ref[...]) * _dot(yb.astype(BF16), wb_ref[...]))
    x1 = x_ref[...] + gt1_ref[...] * _dot(merged.astype(BF16), wo_ref[...])
    x1_ref[...] = x1
    y = x1 * lax.rsqrt(jnp.mean(x1 * x1, axis=-1, keepdims=True) + NORM_EPS)
    h2 = ((y * g2_ref[...]) * (1.0 + sc2_ref[...]) + sh2_ref[...]).astype(BF16)
    h2_ref[...] = h2
    q_ref[...] = _dot(h2, wq_ref[...])


def _merge(x, ya0, ya1, hb0, hb1, po, ga, gb, wa, wb, wo, gt1, g2, sh2, sc2, wq, tm):
    t = x.shape[0]
    row = lambda i: (i, 0)
    fix = lambda i: (0, 0)
    rs = lambda n: pl.BlockSpec((tm, n), row)
    vec = pl.BlockSpec((1, D_MODEL), fix)
    nq = wq.shape[1]
    return pl.pallas_call(
        _merge_kernel,
        grid=(t // tm,),
        in_specs=[rs(D_MODEL), rs(D_RWKV), rs(D_RWKV), rs(D_MLSTM), rs(D_MLSTM), rs(D_MLSTM), rs(D_MODEL), rs(D_MODEL),
                  pl.BlockSpec(wa.shape, fix), pl.BlockSpec(wb.shape, fix), pl.BlockSpec(wo.shape, fix),
                  vec, vec, vec, vec, pl.BlockSpec(wq.shape, fix)],
        out_specs=[rs(D_MODEL), rs(D_MODEL), rs(nq)],
        out_shape=[jax.ShapeDtypeStruct((t, D_MODEL), F32), jax.ShapeDtypeStruct((t, D_MODEL), BF16),
                   jax.ShapeDtypeStruct((t, nq), F32)],
        compiler_params=_params(("parallel",)),
        name="merge",
    )(x, ya0, ya1, hb0, hb1, po, ga, gb, wa, wb, wo, gt1, g2, sh2, sc2, wq)


def _top_rows(vals, k):
    n = vals.shape[0]
    idx = lax.broadcasted_iota(jnp.int32, vals.shape, 0)
    out = []
    for _ in range(k):
        m = jnp.max(vals, axis=0, keepdims=True)
        first = jnp.min(jnp.where(vals == m, idx, n), axis=0, keepdims=True)
        out.append(m)
        vals = jnp.where(idx == first, -jnp.inf, vals)
    return out


def _router_kernel(q_ref, keys_ref, s2_ref, e2_ref, th_ref, e1_ref):
    qb = q_ref[...].astype(BF16)
    s1 = _dot_nt(keys_ref[0, 0], qb[:, :KEY_DIM])
    s2 = _dot_nt(keys_ref[0, 1], qb[:, KEY_DIM:])
    top1 = _top_rows(s1, PEER_TOPK)
    top2 = _top_rows(s2, PEER_TOPK)
    b_mat = jnp.concatenate(top2, axis=0)
    cand = jnp.concatenate([a + b_mat for a in top1], axis=0)
    tau = _top_rows(cand, PEER_TOPK)[-1]
    mx = top1[0] + top2[0]
    zsum = jnp.sum(jnp.where(cand >= tau, jnp.exp(cand - mx), 0.0), axis=0, keepdims=True)
    th = jnp.full(s1.shape, jnp.inf, F32)
    for b in top2:
        th = jnp.where(s1 + b >= tau, b, th)
    s2_ref[0] = s2
    e2_ref[0] = jnp.exp(s2 - top2[0])
    th_ref[0] = th
    e1_ref[0] = jnp.exp(s1 - top1[0]) / zsum


def _router(q, keys, tt):
    t = q.shape[0]
    spec = pl.BlockSpec((1, N_KEYS, tt), lambda i, h: (h, 0, i))
    shape = jax.ShapeDtypeStruct((PEER_HEADS, N_KEYS, t), F32)
    return pl.pallas_call(
        _router_kernel,
        grid=(t // tt, PEER_HEADS),
        in_specs=[pl.BlockSpec((tt, PEER_QDIM), lambda i, h: (i, h)),
                  pl.BlockSpec((1, 2, N_KEYS, KEY_DIM), lambda i, h: (h, 0, 0, 0))],
        out_specs=[spec] * 4,
        out_shape=[shape] * 4,
        compiler_params=_params(("parallel", "parallel")),
        name="router",
    )(q, keys)


def _experts_kernel(na, h2_ref, u_ref, vt_ref, s2_ref, e2_ref, th_ref, e1_ref, x1_ref, gt2_ref, fg_ref, o_ref, acc_ref):
    e = pl.program_id(1)

    @pl.when(e == 0)
    def _():
        acc_ref[...] = jnp.zeros_like(acc_ref)

    act = _dot_nt(u_ref[...], h2_ref[...])
    gl = 0.5 * act * (1.0 + lax.erf(act * (2.0 ** -0.5)))
    parts = []
    for ai in range(na):
        a = e * na + ai
        gate = jnp.zeros((N_KEYS, act.shape[1]), F32)
        for h in range(PEER_HEADS):
            thr = th_ref[h, pl.ds(a, 1), :]
            e1r = e1_ref[h, pl.ds(a, 1), :]
            gate = gate + jnp.where(s2_ref[h] >= thr, e2_ref[h], 0.0) * e1r
        parts.append((gate * gl[ai * N_KEYS:(ai + 1) * N_KEYS]).astype(BF16))
    w = jnp.concatenate(parts, axis=0) if na > 1 else parts[0]
    acc_ref[...] += _dot(vt_ref[...], w)

    @pl.when(e == pl.num_programs(1) - 1)
    def _():
        x2 = x1_ref[...] + gt2_ref[...] * acc_ref[...].T
        y = x2 * lax.rsqrt(jnp.mean(x2 * x2, axis=-1, keepdims=True) + NORM_EPS)
        o_ref[...] = y * fg_ref[...]


def _experts(h2, u, vt, s2, e2, th, e1, x1, gt2, fg, tt, et):
    t = h2.shape[0]
    ne = u.shape[0]
    tok = lambda i, e: (i, 0)
    fix = lambda i, e: (0, 0)
    rt = pl.BlockSpec((PEER_HEADS, N_KEYS, tt), lambda i, e: (0, 0, i))
    return pl.pallas_call(
        functools.partial(_experts_kernel, et // N_KEYS),
        grid=(t // tt, ne // et),
        in_specs=[pl.BlockSpec((tt, D_MODEL), tok),
                  pl.BlockSpec((et, D_MODEL), lambda i, e: (e, 0)),
                  pl.BlockSpec((D_MODEL, et), lambda i, e: (0, e)),
                  rt, rt, rt, rt,
                  pl.BlockSpec((tt, D_MODEL), tok),
                  pl.BlockSpec((1, D_MODEL), fix), pl.BlockSpec((1, D_MODEL), fix)],
        out_specs=pl.BlockSpec((tt, D_MODEL), tok),
        out_shape=jax.ShapeDtypeStruct((t, D_MODEL), F32),
        scratch_shapes=[pltpu.VMEM((D_MODEL, tt), F32)],
        compiler_params=_params(("parallel", "arbitrary")),
        name="experts",
    )(h2, u, vt, s2, e2, th, e1, x1, gt2, fg)


def _tile(n, pref):
    return pref if n % pref == 0 else n


def kernel(x, c, ctx, c_ctx, ada_w, ada_b, norm1_g, w_in, rwkv_conv, rwkv_w0, rwkv_w_up, rwkv_a0, rwkv_a_up, rwkv_g_up, rwkv_k_k, rwkv_k_a, rwkv_r_k, rwkv_ln_w, rwkv_ln_b, mlstm_conv, mlstm_i_b, mlstm_f_b, mlstm_norm_g, w_branch_a, w_branch_b, w_out, norm2_g, peer_wq, peer_keys, peer_u, peer_v, final_g):
    assert x.shape[0] == 1 and ada_w.shape[0] == 1, "one layer, batch 1"
    t, tc = x.shape[1], ctx.shape[1]
    rows = t // GRID_W
    xs, cs = x[0], ctx[0]

    cc = jnp.zeros((8, D_MODEL), F32).at[0].set(c[0]).at[1].set(c_ctx)
    mods = _ada(cc, ada_w[0], ada_b[0][None])
    sh1, sc1, gt1, sh2, sc2, gt2 = [mods[0:1, i * D_MODEL:(i + 1) * D_MODEL] for i in range(N_MOD)]
    csh1, csc1 = mods[1:2, 0:D_MODEL], mods[1:2, D_MODEL:2 * D_MODEL]

    w = w_in[0]
    o = 0
    parts = []
    for n in (RWKV_COLS, 2 * D_MLSTM_QK, D_MLSTM, 2 * N_DIR * MLSTM_HEADS, D_MLSTM, D_MODEL, D_MODEL):
        parts.append(w[:, o:o + n])
        o += n
    parts[3] = jnp.pad(parts[3], ((0, 0), (0, LANES - parts[3].shape[1])))
    weights = [p.astype(BF16) for p in parts]
    g1 = norm1_g[0][None]

    rw_prm = dict(conv=rwkv_conv[0], w0=rwkv_w0[0], w_up=rwkv_w_up[0].astype(BF16), a0=rwkv_a0[0],
                  a_up=rwkv_a_up[0].astype(BF16), k_k=rwkv_k_k[0][None], k_a=rwkv_k_a[0][None], r_k=rwkv_r_k[0][None],
                  ln_w=rwkv_ln_w[0][None], ln_b=rwkv_ln_b[0][None], g_up=rwkv_g_up[0].astype(BF16))
    bias = jnp.concatenate([mlstm_i_b[0].reshape(-1), mlstm_f_b[0].reshape(-1)])
    ml_prm = dict(conv=mlstm_conv[0], bias=jnp.pad(bias, (0, LANES - bias.shape[0]))[None], norm_g=mlstm_norm_g[0][None])
    nu = N_DIR * MLSTM_HEADS

    p_rw, p_qk, p_mv, p_if, _, _, _ = _proj(cs, g1, csh1, csc1, weights, _tile(tc, 256))
    s_zero = jnp.zeros((N_DIR, RWKV_PAIRS, LANES, LANES), F32)
    _, _, rw_state = _rwkv(p_rw, rw_prm, s_zero, tc, tc)
    m_zero = (jnp.zeros((nu, MLSTM_DQK, MLSTM_DV), F32), jnp.zeros((nu, MLSTM_DQK), F32), jnp.zeros((nu, LANES), F32))
    _, _, ml_state = _mlstm(p_qk, p_mv, p_if, ml_prm, m_zero, tc, 1)

    p_rw, p_qk, p_mv, p_if, p_o, p_ga, p_gb = _proj(xs, g1, sh1, sc1, weights, _tile(t, 256))
    ya0, ya1, _ = _rwkv(p_rw, rw_prm, rw_state, _tile(t, 256), GRID_W)
    hb0, hb1, _ = _mlstm(p_qk, p_mv, p_if, ml_prm, ml_state, rows, GRID_W)

    x1, h2, q = _merge(xs, ya0, ya1, hb0, hb1, p_o, p_ga, p_gb, w_branch_a[0].astype(BF16), w_branch_b[0].astype(BF16),
                       w_out[0].astype(BF16), gt1, norm2_g[0][None], sh2, sc2, peer_wq[0].astype(BF16), _tile(t, 256))
    s2, e2, th, e1 = _router(q, peer_keys[0].astype(BF16), _tile(t, 256))
    out = _experts(h2, peer_u[0].astype(BF16), peer_v[0].astype(BF16).T, s2, e2, th, e1, x1, gt2, final_g[None],
                   _tile(t, 512), 512)
    return out[None]
```

```python
import functools

import jax
import jax.numpy as jnp
from jax import lax
from jax.experimental import pallas as pl
from jax.experimental.pallas import tpu as pltpu

F32 = jnp.float32
BF16 = jnp.bfloat16

D_MODEL = 1024
GRID_W = 64
N_MOD = 6
NORM_EPS = 1e-6

RWKV_HEAD = 64
RWKV_HEADS = 8
D_RWKV = RWKV_HEADS * RWKV_HEAD
LORA_W = 64
LORA_A = 64
LORA_G = 128
RWKV_COLS = 3 * D_RWKV + LORA_W + LORA_A + LORA_G
GN_EPS = 64e-5
RWKV_PAIRS = RWKV_HEADS // 2

MLSTM_HEADS = 4
MLSTM_DQK = 64
MLSTM_DV = 128
D_MLSTM_QK = MLSTM_HEADS * MLSTM_DQK
D_MLSTM = MLSTM_HEADS * MLSTM_DV
N_DIR = 2
CHUNK = 64

PEER_HEADS = 8
N_KEYS = 128
PEER_TOPK = 16
KEY_DIM = 128
PEER_QDIM = 2 * KEY_DIM

LANES = 128
VMEM_LIMIT = 56 * 1024 * 1024

_HI = lax.Precision.HIGHEST


def _dot(a, b, precision=None):
    return jnp.dot(a, b, preferred_element_type=F32, precision=precision)


def _dot_nt(a, b):
    return lax.dot_general(a, b, (((1,), (1,)), ((), ())), preferred_element_type=F32)


def _dot_tn(a, b):
    return lax.dot_general(a, b, (((0,), (0,)), ((), ())), preferred_element_type=F32)


def _sigmoid(x):
    return 1.0 / (1.0 + jnp.exp(-x))


def _params(sem):
    return pltpu.CompilerParams(dimension_semantics=sem, vmem_limit_bytes=VMEM_LIMIT)


def _scan_rows(x, reverse):
    n = x.shape[0]
    row = lax.broadcasted_iota(jnp.int32, x.shape, 0)
    d = 1
    while d < n:
        if reverse:
            x = x + jnp.where(row < n - d, pltpu.roll(x, n - d, axis=0), 0.0)
        else:
            x = x + jnp.where(row >= d, pltpu.roll(x, d, axis=0), 0.0)
        d *= 2
    return x


def _conv3_rows(x, w, period):
    n = x.shape[0]
    pos = lax.broadcasted_iota(jnp.int32, x.shape, 0) % period
    prev = jnp.where(pos == 0, 0.0, pltpu.roll(x, 1, axis=0))
    nxt = jnp.where(pos == period - 1, 0.0, pltpu.roll(x, n - 1, axis=0))
    return w[0:1] * prev + w[1:2] * x + w[2:3] * nxt


def _ada_kernel(c_ref, w_ref, b_ref, o_ref):
    c = c_ref[...]
    s = c * _sigmoid(c)
    o_ref[...] = _dot(s, w_ref[...], precision=_HI) + b_ref[...]


def _ada(cc, w, b):
    n = w.shape[1]
    bn = n // 4
    return pl.pallas_call(
        _ada_kernel,
        grid=(n // bn,),
        in_specs=[pl.BlockSpec((8, D_MODEL), lambda i: (0, 0)),
                  pl.BlockSpec((D_MODEL, bn), lambda i: (0, i)),
                  pl.BlockSpec((1, bn), lambda i: (0, i))],
        out_specs=pl.BlockSpec((8, bn), lambda i: (0, i)),
        out_shape=jax.ShapeDtypeStruct((8, n), F32),
        compiler_params=_params(("arbitrary",)),
        name="ada",
    )(cc, w, b)


def _proj_kernel(nw, x_ref, g_ref, sh_ref, sc_ref, *refs):
    x = x_ref[...]
    y = x * lax.rsqrt(jnp.mean(x * x, axis=-1, keepdims=True) + NORM_EPS)
    h = (y * g_ref[...]) * (1.0 + sc_ref[...]) + sh_ref[...]
    hb = h.astype(BF16)
    for w_ref, o_ref in zip(refs[:nw], refs[nw:]):
        o_ref[...] = _dot(hb, w_ref[...])


def _proj(x, g, shift, scale, weights, tm):
    t = x.shape[0]
    row = lambda i: (i, 0)
    fix = lambda i: (0, 0)
    in_specs = [pl.BlockSpec((tm, D_MODEL), row)] + [pl.BlockSpec((1, D_MODEL), fix)] * 3
    in_specs += [pl.BlockSpec(w.shape, fix) for w in weights]
    return pl.pallas_call(
        functools.partial(_proj_kernel, len(weights)),
        grid=(t // tm,),
        in_specs=in_specs,
        out_specs=[pl.BlockSpec((tm, w.shape[1]), row) for w in weights],
        out_shape=[jax.ShapeDtypeStruct((t, w.shape[1]), F32) for w in weights],
        compiler_params=_params(("parallel",)),
        name="proj",
    )(x, g, shift, scale, *weights)


def _rwkv_prep(z, f_ref, conv_ref, w0_ref, wup_ref, a0_ref, aup_ref, kk_ref, ka_ref, gup_ref, period, dst):
    f = _conv3_rows(f_ref[...], conv_ref[...], period)
    r = f[:, 0:D_RWKV]
    k = f[:, D_RWKV:2 * D_RWKV]
    v = f[:, 2 * D_RWKV:3 * D_RWKV]
    o = 3 * D_RWKV
    wd = f[:, o:o + LORA_W]
    ad = f[:, o + LORA_W:o + LORA_W + LORA_A]
    gd = f[:, o + LORA_W + LORA_A:o + LORA_W + LORA_A + LORA_G]
    lw = w0_ref[z:z + 1, :] + _dot(jnp.tanh(wd).astype(BF16), wup_ref[z])
    w_log = -(jnp.maximum(-lw, 0.0) + jnp.log(1.0 + jnp.exp(-jnp.abs(lw)))) - 0.5
    a = _sigmoid(a0_ref[z:z + 1, :] + _dot(ad.astype(BF16), aup_ref[z]))
    g = _dot(_sigmoid(gd).astype(BF16), gup_ref[...])
    logw_s, kraw_s, a_s, keff_s, v_s, r_s, g_s = dst
    logw_s[z] = -jnp.exp(w_log)
    kraw_s[z] = k * kk_ref[...]
    a_s[z] = a
    keff_s[z] = k * (1.0 + (a - 1.0) * ka_ref[...])
    v_s[z] = v
    r_s[z] = r
    g_s[z] = g


def _rwkv_chunk(units, masks):
    def st(x, lane_lo):
        return jnp.concatenate([jnp.where(lane_lo, x, 0.0), jnp.where(lane_lo, 0.0, x)], axis=0)

    def mm(p, q):
        return _dot(p.astype(BF16), q.astype(BF16))

    def each(f, *lists):
        return [f(*args) for args in zip(*lists)]

    pre = []
    for z, lw, kraw, a, keff, v, r, rk, lnw, lnb, s_prev in units:
        lane_lo = masks[z][0]
        rev = z == 1
        cum = _scan_rows(lw, rev)
        tot = cum[0:1, :] if rev else cum[CHUNK - 1:CHUNK, :]
        e_in = jnp.exp(cum)
        e_ex = jnp.exp(cum - lw)
        e_ng = jnp.exp(-cum)
        e_rem = jnp.exp(tot - cum)
        kr = st(kraw, lane_lo)
        inv = 1.0 / jnp.maximum(jnp.sqrt(jnp.sum(kr * kr, axis=1, keepdims=True)), 1e-12)
        ka = kraw * a
        pre.append(dict(
            xk=(st(kraw * e_ex, lane_lo) * inv).astype(BF16), xr=st(r * e_in, lane_lo).astype(BF16),
            yk=st(keff * e_ng, lane_lo).astype(BF16), yb=(st(ka * e_ng, lane_lo) * inv).astype(BF16),
            ykg=st(keff * e_rem, lane_lo).astype(BF16), ybg=(st(ka * e_rem, lane_lo) * inv).astype(BF16),
            vs=st(v, lane_lo), gam=jnp.exp(tot),
            bonus=jnp.sum(st(r * keff * rk, lane_lo), axis=1, keepdims=True)))
    zs = [u[0] for u in units]
    s_prev = [u[10] for u in units]
    sb = [s.astype(BF16) for s in s_prev]
    vb = [p["vs"].astype(BF16) for p in pre]

    m = [_dot_nt(jnp.concatenate([p["xk"], p["xr"]], axis=0), jnp.concatenate([p["yk"], p["yb"]], axis=0)) for p in pre]
    akk = [jnp.where(masks[z][2], x[:LANES, :LANES], 0.0).astype(BF16) for z, x in zip(zs, m)]
    ark = [jnp.where(masks[z][3], x[LANES:, :LANES], 0.0).astype(BF16) for z, x in zip(zs, m)]
    arb = [jnp.where(masks[z][3], x[LANES:, LANES:], 0.0).astype(BF16) for z, x in zip(zs, m)]
    a_d = [jnp.where(masks[z][2] & masks[z][4], x[:LANES, LANES:], 0.0) for z, x in zip(zs, m)]
    a_off = [jnp.where(masks[z][2] & jnp.logical_not(masks[z][4]), x[:LANES, LANES:], 0.0) for z, x in zip(zs, m)]
    eye = masks[0][5]

    a2 = each(mm, a_d, a_d)
    t_d = [eye - x for x in a_d]
    a4 = each(mm, a2, a2)
    t_d = each(lambda t, x: t + mm(t, x), t_d, a2)
    a8 = each(mm, a4, a4)
    t_d = each(lambda t, x: t + mm(t, x), t_d, a4)
    t_d = each(lambda t, x: t + mm(t, x), t_d, a8)
    n1 = each(mm, t_d, a_off)
    rhs = [_dot_nt(p["xk"], s) + _dot(k, v) for p, s, k, v in zip(pre, sb, akk, vb)]
    n2 = each(mm, n1, n1)
    n3 = each(mm, n1, n2)
    tmat = each(lambda x1, x2, x3, t: mm(eye - x1 + x2 - x3, t), n1, n2, n3, t_d)

    ub = [x.astype(BF16) for x in each(mm, tmat, rhs)]
    o = [_dot_nt(p["xr"], s) + _dot(k, v) - _dot(b, u) for p, s, k, v, b, u in zip(pre, sb, ark, vb, arb, ub)]
    s_new = [s * p["gam"] + _dot_tn(v, p["ykg"]) - _dot_tn(u, p["ybg"]) for s, p, v, u in zip(s_prev, pre, vb, ub)]

    ys = []
    for unit, p, x in zip(units, pre, o):
        own = masks[unit[0]][1]
        lnw, lnb = unit[8], unit[9]
        mu = jnp.sum(x, axis=1, keepdims=True) * (1.0 / RWKV_HEAD)
        cen = jnp.where(own, x - mu, 0.0)
        var = jnp.sum(cen * cen, axis=1, keepdims=True) * (1.0 / RWKV_HEAD)
        y = cen * lax.rsqrt(var + GN_EPS) * lnw + jnp.where(own, lnb, 0.0) + p["bonus"] * p["vs"]
        ys.append(y[:CHUNK] + y[CHUNK:])
    return ys, s_new


def _rwkv_masks():
    i = lax.broadcasted_iota(jnp.int32, (LANES, LANES), 0)
    j = lax.broadcasted_iota(jnp.int32, (LANES, LANES), 1)
    lane_lo = lax.broadcasted_iota(jnp.int32, (CHUNK, LANES), 1) < RWKV_HEAD
    same = (i // CHUNK) == (j // CHUNK)
    eye = jnp.where(i == j, 1.0, 0.0).astype(F32)
    diag16 = (i // 16) == (j // 16)
    out = []
    for z in range(N_DIR):
        before = (j % CHUNK) > (i % CHUNK) if z == 1 else (j % CHUNK) < (i % CHUNK)
        strict = same & before
        incl = same & (before | (i == j))
        out.append((lane_lo, same, strict, incl, diag16, eye))
    return out


def _rwkv_kernel(period, nch, f0_ref, f1_ref, conv_ref, w0_ref, wup_ref, a0_ref, aup_ref, kk_ref, ka_ref, rk_ref,
                 lnw_ref, lnb_ref, gup_ref, sin_ref, y0_ref, y1_ref, sout_ref,
                 s_ref, logw_s, kraw_s, a_s, keff_s, v_s, r_s, g_s):
    step = pl.program_id(0)

    @pl.when(step == 0)
    def _():
        s_ref[...] = sin_ref[...]

    dst = (logw_s, kraw_s, a_s, keff_s, v_s, r_s, g_s)
    for z, f_ref in ((0, f0_ref), (1, f1_ref)):
        _rwkv_prep(z, f_ref, conv_ref, w0_ref, wup_ref, a0_ref, aup_ref, kk_ref, ka_ref, gup_ref, period, dst)

    masks = _rwkv_masks()
    y_refs = (y0_ref, y1_ref)

    def chunk_body(ci, carry):
        units, where = [], []
        for z in range(N_DIR):
            r0 = pl.multiple_of((nch - 1 - ci if z == 1 else ci) * CHUNK, CHUNK)
            rows = pl.ds(r0, CHUNK)
            for p in range(RWKV_PAIRS):
                ls = slice(p * LANES, (p + 1) * LANES)
                units.append((z, logw_s[z, rows, ls], kraw_s[z, rows, ls], a_s[z, rows, ls], keff_s[z, rows, ls],
                              v_s[z, rows, ls], r_s[z, rows, ls], rk_ref[:, ls], lnw_ref[:, ls], lnb_ref[:, ls],
                              s_ref[z, p]))
                where.append((z, p, rows, ls))
        ys, s_new = _rwkv_chunk(units, masks)
        for (z, p, rows, ls), y, s in zip(where, ys, s_new):
            s_ref[z, p] = s
            y_refs[z][rows, ls] = y * g_s[z, rows, ls]
        return carry

    lax.fori_loop(0, nch, chunk_body, 0)

    @pl.when(step == pl.num_programs(0) - 1)
    def _():
        sout_ref[...] = s_ref[...]


def _rwkv(feat, prm, s_init, tb, period):
    t = feat.shape[0]
    nb = t // tb
    nch = tb // CHUNK
    fix2 = lambda i: (0, 0)
    fix3 = lambda i: (0, 0, 0)
    fix4 = lambda i: (0, 0, 0, 0)
    fwd = lambda i: (i, 0)
    bwd = lambda i: (nb - 1 - i, 0)
    vec = pl.BlockSpec((1, D_RWKV), fix2)
    state = pl.BlockSpec((N_DIR, RWKV_PAIRS, LANES, LANES), fix4)
    big = pltpu.VMEM((N_DIR, tb, D_RWKV), F32)
    return pl.pallas_call(
        functools.partial(_rwkv_kernel, period, nch),
        grid=(nb,),
        in_specs=[pl.BlockSpec((tb, RWKV_COLS), fwd), pl.BlockSpec((tb, RWKV_COLS), bwd),
                  pl.BlockSpec((3, RWKV_COLS), fix2),
                  pl.BlockSpec((N_DIR, D_RWKV), fix2), pl.BlockSpec((N_DIR, LORA_W, D_RWKV), fix3),
                  pl.BlockSpec((N_DIR, D_RWKV), fix2), pl.BlockSpec((N_DIR, LORA_A, D_RWKV), fix3),
                  vec, vec, vec, vec, vec,
                  pl.BlockSpec((LORA_G, D_RWKV), fix2), state],
        out_specs=[pl.BlockSpec((tb, D_RWKV), fwd), pl.BlockSpec((tb, D_RWKV), bwd), state],
        out_shape=[jax.ShapeDtypeStruct((t, D_RWKV), F32), jax.ShapeDtypeStruct((t, D_RWKV), F32),
                   jax.ShapeDtypeStruct((N_DIR, RWKV_PAIRS, LANES, LANES), F32)],
        scratch_shapes=[pltpu.VMEM((N_DIR, RWKV_PAIRS, LANES, LANES), F32)] + [big] * 7,
        compiler_params=_params(("arbitrary",)),
        name="rwkv",
    )(feat, feat, prm["conv"], prm["w0"], prm["w_up"], prm["a0"], prm["a_up"], prm["k_k"], prm["k_a"], prm["r_k"],
      prm["ln_w"], prm["ln_b"], prm["g_up"], s_init)


def _mlstm_kernel(nch, qk0_ref, qk1_ref, mv0_ref, mv1_ref, if0_ref, if1_ref, conv_ref, bias_ref, ng_ref,
                  cin_ref, nin_ref, min_ref, h0_ref, h1_ref, cout_ref, nout_ref, mout_ref,
                  c_s, n_s, m_s, qk_s, gi_s, lf_s):
    step = pl.program_id(0)
    rows_n = qk0_ref.shape[0]

    @pl.when(step == 0)
    def _():
        c_s[...] = cin_ref[...]
        n_s[...] = nin_ref[...]
        m_s[...] = min_ref[...]

    for z, (qk_ref, if_ref) in enumerate(((qk0_ref, if0_ref), (qk1_ref, if1_ref))):
        qk = _conv3_rows(qk_ref[...], conv_ref[...], rows_n)
        qk_s[z] = qk * _sigmoid(qk)
        gate = if_ref[...] + bias_ref[...]
        gi_s[z] = gate
        lf_s[z] = jnp.minimum(gate, 0.0) - jnp.log(1.0 + jnp.exp(-jnp.abs(gate)))

    ti = lax.broadcasted_iota(jnp.int32, (CHUNK, CHUNK), 0)
    si = lax.broadcasted_iota(jnp.int32, (CHUNK, CHUNK), 1)
    causal = (si <= ti, si >= ti)
    mv_refs = (mv0_ref, mv1_ref)
    h_refs = (h0_ref, h1_ref)
    nh = MLSTM_HEADS

    def chunk_body(ci, carry):
        us = []
        for z in range(N_DIR):
            rev = z == 1
            r0 = pl.multiple_of((nch - 1 - ci if rev else ci) * CHUNK, CHUNK)
            rows = pl.ds(r0, CHUNK)
            gi = gi_s[z, rows, :]
            bcum = _scan_rows(lf_s[z, rows, :], rev)
            gi_t = gi.T
            bcum_t = bcum.T
            for h in range(nh):
                u = z * nh + h
                bcol = bcum[:, 2 * nh + u:2 * nh + u + 1]
                us.append(dict(
                    z=z, h=h, u=u, rows=rows, icol=gi[:, u:u + 1], bcol=bcol, irow=gi_t[u:u + 1, :],
                    brow=bcum_t[2 * nh + u:2 * nh + u + 1, :], bend=bcol[0:1, :] if rev else bcol[CHUNK - 1:CHUNK, :],
                    q=qk_s[z, rows, h * MLSTM_DQK:(h + 1) * MLSTM_DQK],
                    k=qk_s[z, rows, D_MLSTM_QK + h * MLSTM_DQK:D_MLSTM_QK + (h + 1) * MLSTM_DQK] * (MLSTM_DQK ** -0.5),
                    v=mv_refs[z][rows, h * MLSTM_DV:(h + 1) * MLSTM_DV],
                    ct=c_s[u], nrow=n_s[u:u + 1, :], mprev=m_s[u:u + 1, 0:1]))
        qb = [d["q"].astype(BF16) for d in us]
        kb = [d["k"].astype(BF16) for d in us]
        qk = [_dot_nt(x, y) for x, y in zip(qb, kb)]
        qc = [_dot(x, d["ct"].astype(BF16)) for x, d in zip(qb, us)]
        log_d = [jnp.where(causal[d["z"]], d["bcol"] + (d["irow"] - d["brow"]), -jnp.inf) for d in us]
        g_end = [d["bend"] - d["bcol"] + d["icol"] for d in us]
        inter = [d["bcol"] + d["mprev"] for d in us]
        ld_max = [jnp.max(x, axis=1, keepdims=True) for x in log_d]
        ge_max = [jnp.max(x, axis=0, keepdims=True) for x in g_end]
        qn = [jnp.sum(d["q"] * d["nrow"], axis=1, keepdims=True) for d in us]
        m_row = [jnp.maximum(x, y) for x, y in zip(inter, ld_max)]
        m_new = [jnp.maximum(d["bend"] + d["mprev"], x) for d, x in zip(us, ge_max)]
        w_end = [jnp.exp(x - y) for x, y in zip(g_end, m_new)]
        dc = [_dot_tn(x, (d["v"] * w).astype(BF16)) for x, d, w in zip(kb, us, w_end)]
        smat = [x * jnp.exp(y - z) for x, y, z in zip(qk, log_d, m_row)]
        sv = [_dot(x.astype(BF16), d["v"].astype(BF16)) for x, d in zip(smat, us)]
        a_int = [jnp.exp(x - y) for x, y in zip(inter, m_row)]
        ssum = [jnp.sum(x, axis=1, keepdims=True) for x in smat]
        hh = [(y + a * w) / jnp.maximum(jnp.abs(sm + a * n), jnp.exp(-mr))
              for y, a, w, sm, n, mr in zip(sv, a_int, qc, ssum, qn, m_row)]
        hms = [jnp.mean(x * x, axis=1, keepdims=True) for x in hh]
        for d, x, ms, w, mn, dcu in zip(us, hh, hms, w_end, m_new, dc):
            u, h = d["u"], d["h"]
            cols = slice(h * MLSTM_DV, (h + 1) * MLSTM_DV)
            h_refs[d["z"]][d["rows"], cols] = x * lax.rsqrt(ms + NORM_EPS) * ng_ref[:, cols]
            keep = jnp.exp(d["bend"] + d["mprev"] - mn)
            c_s[u] = keep * d["ct"] + dcu
            n_s[u:u + 1, :] = keep * d["nrow"] + jnp.sum(d["k"] * w, axis=0, keepdims=True)
            m_s[u:u + 1, :] = jnp.broadcast_to(mn, (1, LANES))
        return carry

    lax.fori_loop(0, nch, chunk_body, 0)

    @pl.when(step == pl.num_programs(0) - 1)
    def _():
        cout_ref[...] = c_s[...]
        nout_ref[...] = n_s[...]
        mout_ref[...] = m_s[...]


def _mlstm(qk, mv, gates, prm, state, rows, ncol):
    nch = rows // CHUNK
    nu = N_DIR * MLSTM_HEADS
    qk2 = qk.reshape(rows, ncol * 2 * D_MLSTM_QK)
    mv2 = mv.reshape(rows, ncol * D_MLSTM)
    g2 = gates.reshape(rows, ncol * LANES)
    fwd = lambda i: (0, i)
    bwd = lambda i: (0, ncol - 1 - i)
    fix2 = lambda i: (0, 0)
    fix3 = lambda i: (0, 0, 0)
    cspec = pl.BlockSpec((nu, MLSTM_DQK, MLSTM_DV), fix3)
    nspec = pl.BlockSpec((nu, MLSTM_DQK), fix2)
    mspec = pl.BlockSpec((nu, LANES), fix2)
    wide = lambda m: pl.BlockSpec((rows, 2 * D_MLSTM_QK), m)
    outs = pl.pallas_call(
        functools.partial(_mlstm_kernel, nch),
        grid=(ncol,),
        in_specs=[wide(fwd), wide(bwd), wide(fwd), wide(bwd),
                  pl.BlockSpec((rows, LANES), fwd), pl.BlockSpec((rows, LANES), bwd),
                  pl.BlockSpec((3, 2 * D_MLSTM_QK), fix2), pl.BlockSpec((1, LANES), fix2),
                  pl.BlockSpec((1, D_MLSTM), fix2), cspec, nspec, mspec],
        out_specs=[wide(fwd), wide(bwd), cspec, nspec, mspec],
        out_shape=[jax.ShapeDtypeStruct((rows, ncol * D_MLSTM), F32)] * 2
        + [jax.ShapeDtypeStruct((nu, MLSTM_DQK, MLSTM_DV), F32), jax.ShapeDtypeStruct((nu, MLSTM_DQK), F32),
           jax.ShapeDtypeStruct((nu, LANES), F32)],
        scratch_shapes=[pltpu.VMEM((nu, MLSTM_DQK, MLSTM_DV), F32), pltpu.VMEM((nu, MLSTM_DQK), F32),
                        pltpu.VMEM((nu, LANES), F32), pltpu.VMEM((N_DIR, rows, 2 * D_MLSTM_QK), F32),
                        pltpu.VMEM((N_DIR, rows, LANES), F32), pltpu.VMEM((N_DIR, rows, LANES), F32)],
        compiler_params=_params(("arbitrary",)),
        name="mlstm",
    )(qk2, qk2, mv2, mv2, g2, g2, prm["conv"], prm["bias"], prm["norm_g"], *state)
    h0, h1 = outs[0].reshape(rows * ncol, D_MLSTM), outs[1].reshape(rows * ncol, D_MLSTM)
    return h0, h1, tuple(outs[2:])


def _merge_kernel(x_ref, ya0_ref, ya1_ref, hb0_ref, hb1_ref, po_ref, ga_ref, gb_ref, wa_ref, wb_ref, wo_ref,
                  gt1_ref, g2_ref, sh2_ref, sc2_ref, wq_ref, x1_ref, h2_ref, q_ref):
    ya = ya0_ref[...] + ya1_ref[...]
    yb = (hb0_ref[...] + hb1_ref[...]) * _sigmoid(po_ref[...])
    merged = (_sigmoid(ga_ref[...]) * _dot(ya.astype(BF16), wa_ref[...])
              + _sigmoid(gb_ref[...]) * _dot(yb.astype(BF16), wb_ref[...]))
    x1 = x_ref[...] + gt1_ref[...] * _dot(merged.astype(BF16), wo_ref[...])
    x1_ref[...] = x1
    y = x1 * lax.rsqrt(jnp.mean(x1 * x1, axis=-1, keepdims=True) + NORM_EPS)
    h2 = ((y * g2_ref[...]) * (1.0 + sc2_ref[...]) + sh2_ref[...]).astype(BF16)
    h2_ref[...] = h2
    q_ref[...] = _dot(h2, wq_ref[...])


def _merge(x, ya0, ya1, hb0, hb1, po, ga, gb, wa, wb, wo, gt1, g2, sh2, sc2, wq, tm):
    t = x.shape[0]
    row = lambda i: (i, 0)
    fix = lambda i: (0, 0)
    rs = lambda n: pl.BlockSpec((tm, n), row)
    vec = pl.BlockSpec((1, D_MODEL), fix)
    nq = wq.shape[1]
    return pl.pallas_call(
        _merge_kernel,
        grid=(t // tm,),
        in_specs=[rs(D_MODEL), rs(D_RWKV), rs(D_RWKV), rs(D_MLSTM), rs(D_MLSTM), rs(D_MLSTM), rs(D_MODEL), rs(D_MODEL),
                  pl.BlockSpec(wa.shape, fix), pl.BlockSpec(wb.shape, fix), pl.BlockSpec(wo.shape, fix),
                  vec, vec, vec, vec, pl.BlockSpec(wq.shape, fix)],
        out_specs=[rs(D_MODEL), rs(D_MODEL), rs(nq)],
        out_shape=[jax.ShapeDtypeStruct((t, D_MODEL), F32), jax.ShapeDtypeStruct((t, D_MODEL), BF16),
                   jax.ShapeDtypeStruct((t, nq), F32)],
        compiler_params=_params(("parallel",)),
        name="merge",
    )(x, ya0, ya1, hb0, hb1, po, ga, gb, wa, wb, wo, gt1, g2, sh2, sc2, wq)


def _sort_pairs(n):
    pairs = []
    t = max(1, (n - 1).bit_length())
    p = 1 << (t - 1)
    while p > 0:
        q, r, d = 1 << (t - 1), 0, p
        while d > 0:
            pairs += [(i, i + d) for i in range(n - d) if (i & p) == r]
            d, q, r = q - p, q >> 1, p
        p >>= 1
    return pairs


def _bitonic_desc(c):
    n = len(c)
    d = n // 2
    while d > 0:
        for i in range(n):
            if (i & d) == 0:
                c[i], c[i + d] = jnp.maximum(c[i], c[i + d]), jnp.minimum(c[i], c[i + d])
        d //= 2
    return c


def _merge_top(x, y):
    n = len(x)
    return _bitonic_desc([jnp.maximum(x[i], y[n - 1 - i]) for i in range(n)])


def _top16_levels(scores):
    x = [scores[8 * i:8 * (i + 1), :] for i in range(N_KEYS // 8)]
    for i, j in _sort_pairs(len(x)):
        x[i], x[j] = jnp.maximum(x[i], x[j]), jnp.minimum(x[i], x[j])
    for shift in (4, 2, 1):
        x = _merge_top(x, [pltpu.roll(v, shift, axis=0) for v in x])
    return x


def _router_kernel(q_ref, keys_ref, s2_ref, e2_ref, th_ref, e1_ref):
    qb = q_ref[...].astype(BF16)
    s1 = _dot_nt(keys_ref[0, 0], qb[:, :KEY_DIM])
    s2 = _dot_nt(keys_ref[0, 1], qb[:, KEY_DIM:])
    a = _top16_levels(s1)
    b = _top16_levels(s2)
    k = PEER_TOPK
    nj = [k // (i + 1) for i in range(k)]
    cell = {(i, j): a[i] + b[j] for i in range(k) for j in range(nj[i])}
    ninf = jnp.full(a[0].shape, -jnp.inf, F32)
    pad = lambda lst: lst + [ninf] * (k - len(lst))
    lists = [pad([cell[(i, j)] for j in range(nj[i])]) for i in range(4)]
    lists += [pad([cell[(i, j)] for i in range(4, k) if j < nj[i]]) for j in range(3)]
    top = lists[0]
    for other in lists[1:]:
        top = _merge_top(top, other)
    tau = top[k - 1]
    mx = cell[(0, 0)]
    zsum = jnp.zeros_like(tau)
    th_rank = []
    for i in range(k):
        thr = jnp.full(tau.shape, jnp.inf, F32)
        for j in range(nj[i]):
            sel = cell[(i, j)] >= tau
            zsum = zsum + jnp.where(sel, jnp.exp(cell[(i, j)] - mx), 0.0)
            thr = jnp.where(sel, b[j], thr)
        th_rank.append(thr)
    rz = 1.0 / zsum
    for blk in range(N_KEYS // 8):
        rows = slice(8 * blk, 8 * (blk + 1))
        s1b, s2b = s1[rows, :], s2[rows, :]
        th = jnp.full(s1b.shape, jnp.inf, F32)
        for i in reversed(range(k)):
            th = jnp.where(s1b == a[i], th_rank[i], th)
        s2_ref[0, rows, :] = s2b
        e2_ref[0, rows, :] = jnp.exp(s2b - b[0])
        th_ref[0, rows, :] = th
        e1_ref[0, rows, :] = jnp.exp(s1b - a[0]) * rz


def _router(q, keys, tt):
    t = q.shape[0]
    spec = pl.BlockSpec((1, N_KEYS, tt), lambda i, h: (h, 0, i))
    shape = jax.ShapeDtypeStruct((PEER_HEADS, N_KEYS, t), F32)
    return pl.pallas_call(
        _router_kernel,
        grid=(t // tt, PEER_HEADS),
        in_specs=[pl.BlockSpec((tt, PEER_QDIM), lambda i, h: (i, h)),
                  pl.BlockSpec((1, 2, N_KEYS, KEY_DIM), lambda i, h: (h, 0, 0, 0))],
        out_specs=[spec] * 4,
        out_shape=[shape] * 4,
        compiler_params=_params(("parallel", "parallel")),
        name="router",
    )(q, keys)


EXPERT_SUB = 512


def _experts_kernel(nsub, h2_ref, u_ref, vt_ref, s2_ref, e2_ref, th_ref, e1_ref, x1_ref, gt2_ref, fg_ref, o_ref,
                    acc_ref):
    e = pl.program_id(1)
    na = EXPERT_SUB // N_KEYS

    @pl.when(e == 0)
    def _():
        acc_ref[...] = jnp.zeros_like(acc_ref)

    def first_dot(si):
        return _dot_nt(u_ref[si * EXPERT_SUB:(si + 1) * EXPERT_SUB, :], h2_ref[...])

    act = first_dot(0)
    total = None
    for si in range(nsub):
        nxt = first_dot(si + 1) if si + 1 < nsub else None
        gl = 0.5 * act * (1.0 + lax.erf(act * (2.0 ** -0.5)))
        parts = []
        for ai in range(na):
            a = (e * nsub + si) * na + ai
            gate = jnp.zeros((N_KEYS, act.shape[1]), F32)
            for h in range(PEER_HEADS):
                thr = th_ref[h, pl.ds(a, 1), :]
                e1r = e1_ref[h, pl.ds(a, 1), :]
                gate = gate + jnp.where(s2_ref[h] >= thr, e2_ref[h], 0.0) * e1r
            parts.append((gate * gl[ai * N_KEYS:(ai + 1) * N_KEYS]).astype(BF16))
        w = jnp.concatenate(parts, axis=0)
        part = _dot(vt_ref[:, si * EXPERT_SUB:(si + 1) * EXPERT_SUB], w)
        total = part if total is None else total + part
        act = nxt
    acc_ref[...] += total

    @pl.when(e == pl.num_programs(1) - 1)
    def _():
        x2 = x1_ref[...] + gt2_ref[...] * acc_ref[...].T
        y = x2 * lax.rsqrt(jnp.mean(x2 * x2, axis=-1, keepdims=True) + NORM_EPS)
        o_ref[...] = y * fg_ref[...]


def _experts(h2, u, vt, s2, e2, th, e1, x1, gt2, fg, tt, et):
    t = h2.shape[0]
    ne = u.shape[0]
    tok = lambda i, e: (i, 0)
    fix = lambda i, e: (0, 0)
    rt = pl.BlockSpec((PEER_HEADS, N_KEYS, tt), lambda i, e: (0, 0, i))
    return pl.pallas_call(
        functools.partial(_experts_kernel, et // EXPERT_SUB),
        grid=(t // tt, ne // et),
        in_specs=[pl.BlockSpec((tt, D_MODEL), tok),
                  pl.BlockSpec((et, D_MODEL), lambda i, e: (e, 0)),
                  pl.BlockSpec((D_MODEL, et), lambda i, e: (0, e)),
                  rt, rt, rt, rt,
                  pl.BlockSpec((tt, D_MODEL), tok),
                  pl.BlockSpec((1, D_MODEL), fix), pl.BlockSpec((1, D_MODEL), fix)],
        out_specs=pl.BlockSpec((tt, D_MODEL), tok),
        out_shape=jax.ShapeDtypeStruct((t, D_MODEL), F32),
        scratch_shapes=[pltpu.VMEM((D_MODEL, tt), F32)],
        compiler_params=_params(("parallel", "arbitrary")),
        name="experts",
    )(h2, u, vt, s2, e2, th, e1, x1, gt2, fg)


def _tile(n, pref):
    return pref if n % pref == 0 else n


def kernel(x, c, ctx, c_ctx, ada_w, ada_b, norm1_g, w_in, rwkv_conv, rwkv_w0, rwkv_w_up, rwkv_a0, rwkv_a_up, rwkv_g_up, rwkv_k_k, rwkv_k_a, rwkv_r_k, rwkv_ln_w, rwkv_ln_b, mlstm_conv, mlstm_i_b, mlstm_f_b, mlstm_norm_g, w_branch_a, w_branch_b, w_out, norm2_g, peer_wq, peer_keys, peer_u, peer_v, final_g):
    assert x.shape[0] == 1 and ada_w.shape[0] == 1, "one layer, batch 1"
    t, tc = x.shape[1], ctx.shape[1]
    rows = t // GRID_W
    xs, cs = x[0], ctx[0]

    cc = jnp.zeros((8, D_MODEL), F32).at[0].set(c[0]).at[1].set(c_ctx)
    mods = _ada(cc, ada_w[0], ada_b[0][None])
    sh1, sc1, gt1, sh2, sc2, gt2 = [mods[0:1, i * D_MODEL:(i + 1) * D_MODEL] for i in range(N_MOD)]
    csh1, csc1 = mods[1:2, 0:D_MODEL], mods[1:2, D_MODEL:2 * D_MODEL]

    w = w_in[0]
    o = 0
    parts = []
    for n in (RWKV_COLS, 2 * D_MLSTM_QK, D_MLSTM, 2 * N_DIR * MLSTM_HEADS, D_MLSTM, D_MODEL, D_MODEL):
        parts.append(w[:, o:o + n])
        o += n
    parts[3] = jnp.pad(parts[3], ((0, 0), (0, LANES - parts[3].shape[1])))
    weights = [p.astype(BF16) for p in parts]
    g1 = norm1_g[0][None]

    rw_prm = dict(conv=rwkv_conv[0], w0=rwkv_w0[0], w_up=rwkv_w_up[0].astype(BF16), a0=rwkv_a0[0],
                  a_up=rwkv_a_up[0].astype(BF16), k_k=rwkv_k_k[0][None], k_a=rwkv_k_a[0][None], r_k=rwkv_r_k[0][None],
                  ln_w=rwkv_ln_w[0][None], ln_b=rwkv_ln_b[0][None], g_up=rwkv_g_up[0].astype(BF16))
    bias = jnp.concatenate([mlstm_i_b[0].reshape(-1), mlstm_f_b[0].reshape(-1)])
    ml_prm = dict(conv=mlstm_conv[0], bias=jnp.pad(bias, (0, LANES - bias.shape[0]))[None], norm_g=mlstm_norm_g[0][None])
    nu = N_DIR * MLSTM_HEADS

    p_rw, p_qk, p_mv, p_if, _, _, _ = _proj(cs, g1, csh1, csc1, weights, _tile(tc, 256))
    s_zero = jnp.zeros((N_DIR, RWKV_PAIRS, LANES, LANES), F32)
    _, _, rw_state = _rwkv(p_rw, rw_prm, s_zero, tc, tc)
    m_zero = (jnp.zeros((nu, MLSTM_DQK, MLSTM_DV), F32), jnp.zeros((nu, MLSTM_DQK), F32), jnp.zeros((nu, LANES), F32))
    _, _, ml_state = _mlstm(p_qk, p_mv, p_if, ml_prm, m_zero, tc, 1)

    p_rw, p_qk, p_mv, p_if, p_o, p_ga, p_gb = _proj(xs, g1, sh1, sc1, weights, _tile(t, 256))
    ya0, ya1, _ = _rwkv(p_rw, rw_prm, rw_state, _tile(t, 256), GRID_W)
    hb0, hb1, _ = _mlstm(p_qk, p_mv, p_if, ml_prm, ml_state, rows, GRID_W)

    x1, h2, q = _merge(xs, ya0, ya1, hb0, hb1, p_o, p_ga, p_gb, w_branch_a[0].astype(BF16), w_branch_b[0].astype(BF16),
                       w_out[0].astype(BF16), gt1, norm2_g[0][None], sh2, sc2, peer_wq[0].astype(BF16), _tile(t, 256))
    s2, e2, th, e1 = _router(q, peer_keys[0].astype(BF16), _tile(t, 512))
    out = _experts(h2, peer_u[0].astype(BF16), peer_v[0].astype(BF16).T, s2, e2, th, e1, x1, gt2, final_g[None],
                   _tile(t, 512), 2048)
    return out[None]
```

```python
import functools

import jax
import jax.numpy as jnp
from jax import lax
from jax.experimental import pallas as pl
from jax.experimental.pallas import tpu as pltpu

F32 = jnp.float32
BF16 = jnp.bfloat16

D_MODEL = 1024
GRID_W = 64
N_MOD = 6
NORM_EPS = 1e-6

RWKV_HEAD = 64
RWKV_HEADS = 8
D_RWKV = RWKV_HEADS * RWKV_HEAD
LORA_W = 64
LORA_A = 64
LORA_G = 128
RWKV_COLS = 3 * D_RWKV + LORA_W + LORA_A + LORA_G
GN_EPS = 64e-5
RWKV_PAIRS = RWKV_HEADS // 2

MLSTM_HEADS = 4
MLSTM_DQK = 64
MLSTM_DV = 128
D_MLSTM_QK = MLSTM_HEADS * MLSTM_DQK
D_MLSTM = MLSTM_HEADS * MLSTM_DV
N_DIR = 2
CHUNK = 64

PEER_HEADS = 8
N_KEYS = 128
PEER_TOPK = 16
KEY_DIM = 128
PEER_QDIM = 2 * KEY_DIM

LANES = 128
VMEM_LIMIT = 56 * 1024 * 1024

_HI = lax.Precision.HIGHEST


def _dot(a, b, precision=None):
    return jnp.dot(a, b, preferred_element_type=F32, precision=precision)


def _dot_nt(a, b):
    return lax.dot_general(a, b, (((1,), (1,)), ((), ())), preferred_element_type=F32)


def _dot_tn(a, b):
    return lax.dot_general(a, b, (((0,), (0,)), ((), ())), preferred_element_type=F32)


def _sigmoid(x):
    return 1.0 / (1.0 + jnp.exp(-x))


def _params(sem):
    return pltpu.CompilerParams(dimension_semantics=sem, vmem_limit_bytes=VMEM_LIMIT)


def _scan_rows(x, reverse):
    n = x.shape[0]
    row = lax.broadcasted_iota(jnp.int32, x.shape, 0)
    d = 1
    while d < n:
        if reverse:
            x = x + jnp.where(row < n - d, pltpu.roll(x, n - d, axis=0), 0.0)
        else:
            x = x + jnp.where(row >= d, pltpu.roll(x, d, axis=0), 0.0)
        d *= 2
    return x


def _conv3_rows(x, w, period):
    n = x.shape[0]
    pos = lax.broadcasted_iota(jnp.int32, x.shape, 0) % period
    prev = jnp.where(pos == 0, 0.0, pltpu.roll(x, 1, axis=0))
    nxt = jnp.where(pos == period - 1, 0.0, pltpu.roll(x, n - 1, axis=0))
    return w[0:1] * prev + w[1:2] * x + w[2:3] * nxt


def _ada_kernel(c_ref, w_ref, b_ref, o_ref):
    c = c_ref[...]
    s = c * _sigmoid(c)
    o_ref[...] = _dot(s, w_ref[...], precision=_HI) + b_ref[...]


def _ada(cc, w, b):
    n = w.shape[1]
    bn = n // 4
    return pl.pallas_call(
        _ada_kernel,
        grid=(n // bn,),
        in_specs=[pl.BlockSpec((8, D_MODEL), lambda i: (0, 0)),
                  pl.BlockSpec((D_MODEL, bn), lambda i: (0, i)),
                  pl.BlockSpec((1, bn), lambda i: (0, i))],
        out_specs=pl.BlockSpec((8, bn), lambda i: (0, i)),
        out_shape=jax.ShapeDtypeStruct((8, n), F32),
        compiler_params=_params(("arbitrary",)),
        name="ada",
    )(cc, w, b)


def _proj_kernel(nw, x_ref, g_ref, sh_ref, sc_ref, *refs):
    x = x_ref[...]
    y = x * lax.rsqrt(jnp.mean(x * x, axis=-1, keepdims=True) + NORM_EPS)
    h = (y * g_ref[...]) * (1.0 + sc_ref[...]) + sh_ref[...]
    hb = h.astype(BF16)
    for w_ref, o_ref in zip(refs[:nw], refs[nw:]):
        o_ref[...] = _dot(hb, w_ref[...])


def _proj(x, g, shift, scale, weights, tm):
    t = x.shape[0]
    row = lambda i: (i, 0)
    fix = lambda i: (0, 0)
    in_specs = [pl.BlockSpec((tm, D_MODEL), row)] + [pl.BlockSpec((1, D_MODEL), fix)] * 3
    in_specs += [pl.BlockSpec(w.shape, fix) for w in weights]
    return pl.pallas_call(
        functools.partial(_proj_kernel, len(weights)),
        grid=(t // tm,),
        in_specs=in_specs,
        out_specs=[pl.BlockSpec((tm, w.shape[1]), row) for w in weights],
        out_shape=[jax.ShapeDtypeStruct((t, w.shape[1]), F32) for w in weights],
        compiler_params=_params(("parallel",)),
        name="proj",
    )(x, g, shift, scale, *weights)


def _rwkv_prep(z, f_ref, conv_ref, w0_ref, wup_ref, a0_ref, aup_ref, kk_ref, ka_ref, gup_ref, period, dst):
    f = _conv3_rows(f_ref[...], conv_ref[...], period)
    r = f[:, 0:D_RWKV]
    k = f[:, D_RWKV:2 * D_RWKV]
    v = f[:, 2 * D_RWKV:3 * D_RWKV]
    o = 3 * D_RWKV
    wd = f[:, o:o + LORA_W]
    ad = f[:, o + LORA_W:o + LORA_W + LORA_A]
    gd = f[:, o + LORA_W + LORA_A:o + LORA_W + LORA_A + LORA_G]
    lw = w0_ref[z:z + 1, :] + _dot(jnp.tanh(wd).astype(BF16), wup_ref[z])
    w_log = -(jnp.maximum(-lw, 0.0) + jnp.log(1.0 + jnp.exp(-jnp.abs(lw)))) - 0.5
    a = _sigmoid(a0_ref[z:z + 1, :] + _dot(ad.astype(BF16), aup_ref[z]))
    g = _dot(_sigmoid(gd).astype(BF16), gup_ref[...])
    logw_s, kraw_s, a_s, keff_s, v_s, r_s, g_s = dst
    logw_s[z] = -jnp.exp(w_log)
    kraw_s[z] = k * kk_ref[...]
    a_s[z] = a
    keff_s[z] = k * (1.0 + (a - 1.0) * ka_ref[...])
    v_s[z] = v
    r_s[z] = r
    g_s[z] = g


def _rwkv_chunk(units, masks):
    def st(x, lane_lo):
        return jnp.concatenate([jnp.where(lane_lo, x, 0.0), jnp.where(lane_lo, 0.0, x)], axis=0)

    def mm(p, q):
        return _dot(p.astype(BF16), q.astype(BF16))

    def each(f, *lists):
        return [f(*args) for args in zip(*lists)]

    pre = []
    for z, lw, kraw, a, keff, v, r, rk, lnw, lnb, s_prev in units:
        lane_lo = masks[z][0]
        rev = z == 1
        cum = _scan_rows(lw, rev)
        tot = cum[0:1, :] if rev else cum[CHUNK - 1:CHUNK, :]
        e_in = jnp.exp(cum)
        e_ex = jnp.exp(cum - lw)
        e_ng = jnp.exp(-cum)
        e_rem = jnp.exp(tot - cum)
        kr = st(kraw, lane_lo)
        inv = 1.0 / jnp.maximum(jnp.sqrt(jnp.sum(kr * kr, axis=1, keepdims=True)), 1e-12)
        ka = kraw * a
        pre.append(dict(
            xk=(st(kraw * e_ex, lane_lo) * inv).astype(BF16), xr=st(r * e_in, lane_lo).astype(BF16),
            yk=st(keff * e_ng, lane_lo).astype(BF16), yb=(st(ka * e_ng, lane_lo) * inv).astype(BF16),
            ykg=st(keff * e_rem, lane_lo).astype(BF16), ybg=(st(ka * e_rem, lane_lo) * inv).astype(BF16),
            vs=st(v, lane_lo), gam=jnp.exp(tot),
            bonus=jnp.sum(st(r * keff * rk, lane_lo), axis=1, keepdims=True)))
    zs = [u[0] for u in units]
    s_prev = [u[10] for u in units]
    sb = [s.astype(BF16) for s in s_prev]
    vb = [p["vs"].astype(BF16) for p in pre]

    m = [_dot_nt(jnp.concatenate([p["xk"], p["xr"]], axis=0), jnp.concatenate([p["yk"], p["yb"]], axis=0)) for p in pre]
    akk = [jnp.where(masks[z][2], x[:LANES, :LANES], 0.0).astype(BF16) for z, x in zip(zs, m)]
    ark = [jnp.where(masks[z][3], x[LANES:, :LANES], 0.0).astype(BF16) for z, x in zip(zs, m)]
    arb = [jnp.where(masks[z][3], x[LANES:, LANES:], 0.0).astype(BF16) for z, x in zip(zs, m)]
    a_d = [jnp.where(masks[z][2] & masks[z][4], x[:LANES, LANES:], 0.0) for z, x in zip(zs, m)]
    a_off = [jnp.where(masks[z][2] & jnp.logical_not(masks[z][4]), x[:LANES, LANES:], 0.0) for z, x in zip(zs, m)]
    eye = masks[0][5]

    a2 = each(mm, a_d, a_d)
    t_d = [eye - x for x in a_d]
    a4 = each(mm, a2, a2)
    t_d = each(lambda t, x: t + mm(t, x), t_d, a2)
    a8 = each(mm, a4, a4)
    t_d = each(lambda t, x: t + mm(t, x), t_d, a4)
    t_d = each(lambda t, x: t + mm(t, x), t_d, a8)
    n1 = each(mm, t_d, a_off)
    rhs = [_dot_nt(p["xk"], s) + _dot(k, v) for p, s, k, v in zip(pre, sb, akk, vb)]
    n2 = each(mm, n1, n1)
    n3 = each(mm, n1, n2)
    tmat = each(lambda x1, x2, x3, t: mm(eye - x1 + x2 - x3, t), n1, n2, n3, t_d)

    ub = [x.astype(BF16) for x in each(mm, tmat, rhs)]
    o = [_dot_nt(p["xr"], s) + _dot(k, v) - _dot(b, u) for p, s, k, v, b, u in zip(pre, sb, ark, vb, arb, ub)]
    s_new = [s * p["gam"] + _dot_tn(v, p["ykg"]) - _dot_tn(u, p["ybg"]) for s, p, v, u in zip(s_prev, pre, vb, ub)]

    ys = []
    for unit, p, x in zip(units, pre, o):
        own = masks[unit[0]][1]
        lnw, lnb = unit[8], unit[9]
        mu = jnp.sum(x, axis=1, keepdims=True) * (1.0 / RWKV_HEAD)
        cen = jnp.where(own, x - mu, 0.0)
        var = jnp.sum(cen * cen, axis=1, keepdims=True) * (1.0 / RWKV_HEAD)
        y = cen * lax.rsqrt(var + GN_EPS) * lnw + jnp.where(own, lnb, 0.0) + p["bonus"] * p["vs"]
        ys.append(y[:CHUNK] + y[CHUNK:])
    return ys, s_new


def _rwkv_masks():
    i = lax.broadcasted_iota(jnp.int32, (LANES, LANES), 0)
    j = lax.broadcasted_iota(jnp.int32, (LANES, LANES), 1)
    lane_lo = lax.broadcasted_iota(jnp.int32, (CHUNK, LANES), 1) < RWKV_HEAD
    same = (i // CHUNK) == (j // CHUNK)
    eye = jnp.where(i == j, 1.0, 0.0).astype(F32)
    diag16 = (i // 16) == (j // 16)
    out = []
    for z in range(N_DIR):
        before = (j % CHUNK) > (i % CHUNK) if z == 1 else (j % CHUNK) < (i % CHUNK)
        strict = same & before
        incl = same & (before | (i == j))
        out.append((lane_lo, same, strict, incl, diag16, eye))
    return out


def _rwkv_kernel(period, nch, f0_ref, f1_ref, conv_ref, w0_ref, wup_ref, a0_ref, aup_ref, kk_ref, ka_ref, rk_ref,
                 lnw_ref, lnb_ref, gup_ref, sin_ref, y0_ref, y1_ref, sout_ref,
                 s_ref, logw_s, kraw_s, a_s, keff_s, v_s, r_s, g_s):
    step = pl.program_id(0)

    @pl.when(step == 0)
    def _():
        s_ref[...] = sin_ref[...]

    dst = (logw_s, kraw_s, a_s, keff_s, v_s, r_s, g_s)
    for z, f_ref in ((0, f0_ref), (1, f1_ref)):
        _rwkv_prep(z, f_ref, conv_ref, w0_ref, wup_ref, a0_ref, aup_ref, kk_ref, ka_ref, gup_ref, period, dst)

    masks = _rwkv_masks()
    y_refs = (y0_ref, y1_ref)

    def chunk_body(ci, carry):
        units, where = [], []
        for z in range(N_DIR):
            r0 = pl.multiple_of((nch - 1 - ci if z == 1 else ci) * CHUNK, CHUNK)
            rows = pl.ds(r0, CHUNK)
            for p in range(RWKV_PAIRS):
                ls = slice(p * LANES, (p + 1) * LANES)
                units.append((z, logw_s[z, rows, ls], kraw_s[z, rows, ls], a_s[z, rows, ls], keff_s[z, rows, ls],
                              v_s[z, rows, ls], r_s[z, rows, ls], rk_ref[:, ls], lnw_ref[:, ls], lnb_ref[:, ls],
                              s_ref[z, p]))
                where.append((z, p, rows, ls))
        ys, s_new = _rwkv_chunk(units, masks)
        for (z, p, rows, ls), y, s in zip(where, ys, s_new):
            s_ref[z, p] = s
            y_refs[z][rows, ls] = y * g_s[z, rows, ls]
        return carry

    lax.fori_loop(0, nch, chunk_body, 0)

    @pl.when(step == pl.num_programs(0) - 1)
    def _():
        sout_ref[...] = s_ref[...]


def _rwkv(feat, prm, s_init, tb, period):
    t = feat.shape[0]
    nb = t // tb
    nch = tb // CHUNK
    fix2 = lambda i: (0, 0)
    fix3 = lambda i: (0, 0, 0)
    fix4 = lambda i: (0, 0, 0, 0)
    fwd = lambda i: (i, 0)
    bwd = lambda i: (nb - 1 - i, 0)
    vec = pl.BlockSpec((1, D_RWKV), fix2)
    state = pl.BlockSpec((N_DIR, RWKV_PAIRS, LANES, LANES), fix4)
    big = pltpu.VMEM((N_DIR, tb, D_RWKV), F32)
    return pl.pallas_call(
        functools.partial(_rwkv_kernel, period, nch),
        grid=(nb,),
        in_specs=[pl.BlockSpec((tb, RWKV_COLS), fwd), pl.BlockSpec((tb, RWKV_COLS), bwd),
                  pl.BlockSpec((3, RWKV_COLS), fix2),
                  pl.BlockSpec((N_DIR, D_RWKV), fix2), pl.BlockSpec((N_DIR, LORA_W, D_RWKV), fix3),
                  pl.BlockSpec((N_DIR, D_RWKV), fix2), pl.BlockSpec((N_DIR, LORA_A, D_RWKV), fix3),
                  vec, vec, vec, vec, vec,
                  pl.BlockSpec((LORA_G, D_RWKV), fix2), state],
        out_specs=[pl.BlockSpec((tb, D_RWKV), fwd), pl.BlockSpec((tb, D_RWKV), bwd), state],
        out_shape=[jax.ShapeDtypeStruct((t, D_RWKV), F32), jax.ShapeDtypeStruct((t, D_RWKV), F32),
                   jax.ShapeDtypeStruct((N_DIR, RWKV_PAIRS, LANES, LANES), F32)],
        scratch_shapes=[pltpu.VMEM((N_DIR, RWKV_PAIRS, LANES, LANES), F32)] + [big] * 7,
        compiler_params=_params(("arbitrary",)),
        name="rwkv",
    )(feat, feat, prm["conv"], prm["w0"], prm["w_up"], prm["a0"], prm["a_up"], prm["k_k"], prm["k_a"], prm["r_k"],
      prm["ln_w"], prm["ln_b"], prm["g_up"], s_init)


def _mlstm_kernel(nch, qk0_ref, qk1_ref, mv0_ref, mv1_ref, if0_ref, if1_ref, conv_ref, bias_ref, ng_ref,
                  cin_ref, nin_ref, min_ref, h0_ref, h1_ref, cout_ref, nout_ref, mout_ref,
                  c_s, n_s, m_s, qk_s, gi_s, lf_s):
    step = pl.program_id(0)
    rows_n = qk0_ref.shape[0]

    @pl.when(step == 0)
    def _():
        c_s[...] = cin_ref[...]
        n_s[...] = nin_ref[...]
        m_s[...] = min_ref[...]

    for z, (qk_ref, if_ref) in enumerate(((qk0_ref, if0_ref), (qk1_ref, if1_ref))):
        qk = _conv3_rows(qk_ref[...], conv_ref[...], rows_n)
        qk_s[z] = qk * _sigmoid(qk)
        gate = if_ref[...] + bias_ref[...]
        gi_s[z] = gate
        lf_s[z] = jnp.minimum(gate, 0.0) - jnp.log(1.0 + jnp.exp(-jnp.abs(gate)))

    ti = lax.broadcasted_iota(jnp.int32, (CHUNK, CHUNK), 0)
    si = lax.broadcasted_iota(jnp.int32, (CHUNK, CHUNK), 1)
    causal = (si <= ti, si >= ti)
    mv_refs = (mv0_ref, mv1_ref)
    h_refs = (h0_ref, h1_ref)
    nh = MLSTM_HEADS

    def chunk_body(ci, carry):
        us = []
        for z in range(N_DIR):
            rev = z == 1
            r0 = pl.multiple_of((nch - 1 - ci if rev else ci) * CHUNK, CHUNK)
            rows = pl.ds(r0, CHUNK)
            gi = gi_s[z, rows, :]
            bcum = _scan_rows(lf_s[z, rows, :], rev)
            gi_t = gi.T
            bcum_t = bcum.T
            for h in range(nh):
                u = z * nh + h
                bcol = bcum[:, 2 * nh + u:2 * nh + u + 1]
                us.append(dict(
                    z=z, h=h, u=u, rows=rows, icol=gi[:, u:u + 1], bcol=bcol, irow=gi_t[u:u + 1, :],
                    brow=bcum_t[2 * nh + u:2 * nh + u + 1, :], bend=bcol[0:1, :] if rev else bcol[CHUNK - 1:CHUNK, :],
                    q=qk_s[z, rows, h * MLSTM_DQK:(h + 1) * MLSTM_DQK],
                    k=qk_s[z, rows, D_MLSTM_QK + h * MLSTM_DQK:D_MLSTM_QK + (h + 1) * MLSTM_DQK] * (MLSTM_DQK ** -0.5),
                    v=mv_refs[z][rows, h * MLSTM_DV:(h + 1) * MLSTM_DV],
                    ct=c_s[u], nrow=n_s[u:u + 1, :], mprev=m_s[u:u + 1, 0:1]))
        qb = [d["q"].astype(BF16) for d in us]
        kb = [d["k"].astype(BF16) for d in us]
        qk = [_dot_nt(x, y) for x, y in zip(qb, kb)]
        qc = [_dot(x, d["ct"].astype(BF16)) for x, d in zip(qb, us)]
        log_d = [jnp.where(causal[d["z"]], d["bcol"] + (d["irow"] - d["brow"]), -jnp.inf) for d in us]
        g_end = [d["bend"] - d["bcol"] + d["icol"] for d in us]
        inter = [d["bcol"] + d["mprev"] for d in us]
        ld_max = [jnp.max(x, axis=1, keepdims=True) for x in log_d]
        ge_max = [jnp.max(x, axis=0, keepdims=True) for x in g_end]
        qn = [jnp.sum(d["q"] * d["nrow"], axis=1, keepdims=True) for d in us]
        m_row = [jnp.maximum(x, y) for x, y in zip(inter, ld_max)]
        m_new = [jnp.maximum(d["bend"] + d["mprev"], x) for d, x in zip(us, ge_max)]
        w_end = [jnp.exp(x - y) for x, y in zip(g_end, m_new)]
        dc = [_dot_tn(x, (d["v"] * w).astype(BF16)) for x, d, w in zip(kb, us, w_end)]
        smat = [x * jnp.exp(y - z) for x, y, z in zip(qk, log_d, m_row)]
        sv = [_dot(x.astype(BF16), d["v"].astype(BF16)) for x, d in zip(smat, us)]
        a_int = [jnp.exp(x - y) for x, y in zip(inter, m_row)]
        ssum = [jnp.sum(x, axis=1, keepdims=True) for x in smat]
        hh = [(y + a * w) / jnp.maximum(jnp.abs(sm + a * n), jnp.exp(-mr))
              for y, a, w, sm, n, mr in zip(sv, a_int, qc, ssum, qn, m_row)]
        hms = [jnp.mean(x * x, axis=1, keepdims=True) for x in hh]
        for d, x, ms, w, mn, dcu in zip(us, hh, hms, w_end, m_new, dc):
            u, h = d["u"], d["h"]
            cols = slice(h * MLSTM_DV, (h + 1) * MLSTM_DV)
            h_refs[d["z"]][d["rows"], cols] = x * lax.rsqrt(ms + NORM_EPS) * ng_ref[:, cols]
            keep = jnp.exp(d["bend"] + d["mprev"] - mn)
            c_s[u] = keep * d["ct"] + dcu
            n_s[u:u + 1, :] = keep * d["nrow"] + jnp.sum(d["k"] * w, axis=0, keepdims=True)
            m_s[u:u + 1, :] = jnp.broadcast_to(mn, (1, LANES))
        return carry

    lax.fori_loop(0, nch, chunk_body, 0)

    @pl.when(step == pl.num_programs(0) - 1)
    def _():
        cout_ref[...] = c_s[...]
        nout_ref[...] = n_s[...]
        mout_ref[...] = m_s[...]


def _mlstm(qk, mv, gates, prm, state, rows, ncol):
    nch = rows // CHUNK
    nu = N_DIR * MLSTM_HEADS
    qk2 = qk.reshape(rows, ncol * 2 * D_MLSTM_QK)
    mv2 = mv.reshape(rows, ncol * D_MLSTM)
    g2 = gates.reshape(rows, ncol * LANES)
    fwd = lambda i: (0, i)
    bwd = lambda i: (0, ncol - 1 - i)
    fix2 = lambda i: (0, 0)
    fix3 = lambda i: (0, 0, 0)
    cspec = pl.BlockSpec((nu, MLSTM_DQK, MLSTM_DV), fix3)
    nspec = pl.BlockSpec((nu, MLSTM_DQK), fix2)
    mspec = pl.BlockSpec((nu, LANES), fix2)
    wide = lambda m: pl.BlockSpec((rows, 2 * D_MLSTM_QK), m)
    outs = pl.pallas_call(
        functools.partial(_mlstm_kernel, nch),
        grid=(ncol,),
        in_specs=[wide(fwd), wide(bwd), wide(fwd), wide(bwd),
                  pl.BlockSpec((rows, LANES), fwd), pl.BlockSpec((rows, LANES), bwd),
                  pl.BlockSpec((3, 2 * D_MLSTM_QK), fix2), pl.BlockSpec((1, LANES), fix2),
                  pl.BlockSpec((1, D_MLSTM), fix2), cspec, nspec, mspec],
        out_specs=[wide(fwd), wide(bwd), cspec, nspec, mspec],
        out_shape=[jax.ShapeDtypeStruct((rows, ncol * D_MLSTM), F32)] * 2
        + [jax.ShapeDtypeStruct((nu, MLSTM_DQK, MLSTM_DV), F32), jax.ShapeDtypeStruct((nu, MLSTM_DQK), F32),
           jax.ShapeDtypeStruct((nu, LANES), F32)],
        scratch_shapes=[pltpu.VMEM((nu, MLSTM_DQK, MLSTM_DV), F32), pltpu.VMEM((nu, MLSTM_DQK), F32),
                        pltpu.VMEM((nu, LANES), F32), pltpu.VMEM((N_DIR, rows, 2 * D_MLSTM_QK), F32),
                        pltpu.VMEM((N_DIR, rows, LANES), F32), pltpu.VMEM((N_DIR, rows, LANES), F32)],
        compiler_params=_params(("arbitrary",)),
        name="mlstm",
    )(qk2, qk2, mv2, mv2, g2, g2, prm["conv"], prm["bias"], prm["norm_g"], *state)
    h0, h1 = outs[0].reshape(rows * ncol, D_MLSTM), outs[1].reshape(rows * ncol, D_MLSTM)
    return h0, h1, tuple(outs[2:])


def _merge_kernel(x_ref, ya0_ref, ya1_ref, hb0_ref, hb1_ref, po_ref, ga_ref, gb_ref, wa_ref, wb_ref, wo_ref,
                  gt1_ref, g2_ref, sh2_ref, sc2_ref, wq_ref, x1_ref, h2_ref, q_ref):
    ya = ya0_ref[...] + ya1_ref[...]
    yb = (hb0_ref[...] + hb1_ref[...]) * _sigmoid(po_ref[...])
    merged = (_sigmoid(ga_ref[...]) * _dot(ya.astype(BF16), wa_ref[...])
              + _sigmoid(gb_ref[...]) * _dot(yb.astype(BF16), wb_ref[...]))
    x1 = x_ref[...] + gt1_ref[...] * _dot(merged.astype(BF16), wo_ref[...])
    x1_ref[...] = x1
    y = x1 * lax.rsqrt(jnp.mean(x1 * x1, axis=-1, keepdims=True) + NORM_EPS)
    h2 = ((y * g2_ref[...]) * (1.0 + sc2_ref[...]) + sh2_ref[...]).astype(BF16)
    h2_ref[...] = h2
    q_ref[...] = _dot(h2, wq_ref[...])


def _merge(x, ya0, ya1, hb0, hb1, po, ga, gb, wa, wb, wo, gt1, g2, sh2, sc2, wq, tm):
    t = x.shape[0]
    row = lambda i: (i, 0)
    fix = lambda i: (0, 0)
    rs = lambda n: pl.BlockSpec((tm, n), row)
    vec = pl.BlockSpec((1, D_MODEL), fix)
    nq = wq.shape[1]
    return pl.pallas_call(
        _merge_kernel,
        grid=(t // tm,),
        in_specs=[rs(D_MODEL), rs(D_RWKV), rs(D_RWKV), rs(D_MLSTM), rs(D_MLSTM), rs(D_MLSTM), rs(D_MODEL), rs(D_MODEL),
                  pl.BlockSpec(wa.shape, fix), pl.BlockSpec(wb.shape, fix), pl.BlockSpec(wo.shape, fix),
                  vec, vec, vec, vec, pl.BlockSpec(wq.shape, fix)],
        out_specs=[rs(D_MODEL), rs(D_MODEL), rs(nq)],
        out_shape=[jax.ShapeDtypeStruct((t, D_MODEL), F32), jax.ShapeDtypeStruct((t, D_MODEL), BF16),
                   jax.ShapeDtypeStruct((t, nq), F32)],
        compiler_params=_params(("parallel",)),
        name="merge",
    )(x, ya0, ya1, hb0, hb1, po, ga, gb, wa, wb, wo, gt1, g2, sh2, sc2, wq)


def _sort_pairs(n):
    pairs = []
    t = max(1, (n - 1).bit_length())
    p = 1 << (t - 1)
    while p > 0:
        q, r, d = 1 << (t - 1), 0, p
        while d > 0:
            pairs += [(i, i + d) for i in range(n - d) if (i & p) == r]
            d, q, r = q - p, q >> 1, p
        p >>= 1
    return pairs


def _bitonic_desc(c):
    n = len(c)
    d = n // 2
    while d > 0:
        for i in range(n):
            if (i & d) == 0:
                c[i], c[i + d] = jnp.maximum(c[i], c[i + d]), jnp.minimum(c[i], c[i + d])
        d //= 2
    return c


def _merge_top(x, y):
    n = len(x)
    return _bitonic_desc([jnp.maximum(x[i], y[n - 1 - i]) for i in range(n)])


def _top16_levels(scores):
    x = [scores[8 * i:8 * (i + 1), :] for i in range(N_KEYS // 8)]
    for i, j in _sort_pairs(len(x)):
        x[i], x[j] = jnp.maximum(x[i], x[j]), jnp.minimum(x[i], x[j])
    for shift in (4, 2, 1):
        x = _merge_top(x, [pltpu.roll(v, shift, axis=0) for v in x])
    return x


def _router_kernel(q_ref, keys_ref, r2_ref, e2_ref, n1_ref, e1_ref):
    qb = q_ref[...].astype(BF16)
    s1 = _dot_nt(keys_ref[0, 0], qb[:, :KEY_DIM])
    s2 = _dot_nt(keys_ref[0, 1], qb[:, KEY_DIM:])
    a = _top16_levels(s1)
    b = _top16_levels(s2)
    k = PEER_TOPK
    nj = [k // (i + 1) for i in range(k)]
    cell = {(i, j): a[i] + b[j] for i in range(k) for j in range(nj[i])}
    ninf = jnp.full(a[0].shape, -jnp.inf, F32)
    pad = lambda lst: lst + [ninf] * (k - len(lst))
    lists = [pad([cell[(i, j)] for j in range(nj[i])]) for i in range(4)]
    lists += [pad([cell[(i, j)] for i in range(4, k) if j < nj[i]]) for j in range(3)]
    top = lists[0]
    for other in lists[1:]:
        top = _merge_top(top, other)
    tau = top[k - 1]
    mx = cell[(0, 0)]
    zsum = jnp.zeros_like(tau)
    n_rank = []
    for i in range(k):
        cnt = jnp.zeros_like(tau)
        for j in range(nj[i]):
            sel = cell[(i, j)] >= tau
            zsum = zsum + jnp.where(sel, jnp.exp(cell[(i, j)] - mx), 0.0)
            cnt = cnt + jnp.where(sel, 1.0, 0.0)
        n_rank.append(cnt)
    rz = 1.0 / zsum
    for blk in range(N_KEYS // 16):
        r2, e2 = [], []
        for half in range(2):
            rows = slice(16 * blk + 8 * half, 16 * blk + 8 * (half + 1))
            s1b, s2b = s1[rows, :], s2[rows, :]
            n1 = jnp.zeros_like(s1b)
            for i in reversed(range(k)):
                n1 = jnp.where(s1b == a[i], n_rank[i], n1)
            n1_ref[0, rows, :] = n1
            e1_ref[0, rows, :] = jnp.exp(s1b - a[0]) * rz
            rank = jnp.zeros_like(s2b)
            for lvl in b:
                rank = rank + jnp.where(lvl > s2b, 1.0, 0.0)
            r2.append(rank)
            e2.append(jnp.exp(s2b - b[0]))
        rows = slice(16 * blk, 16 * (blk + 1))
        r2_ref[0, rows, :] = jnp.concatenate(r2, axis=0).astype(BF16)
        e2_ref[0, rows, :] = jnp.concatenate(e2, axis=0).astype(BF16)


def _router(q, keys, tt):
    t = q.shape[0]
    spec = pl.BlockSpec((1, N_KEYS, tt), lambda i, h: (h, 0, i))
    shape = lambda dt: jax.ShapeDtypeStruct((PEER_HEADS, N_KEYS, t), dt)
    return pl.pallas_call(
        _router_kernel,
        grid=(t // tt, PEER_HEADS),
        in_specs=[pl.BlockSpec((tt, PEER_QDIM), lambda i, h: (i, h)),
                  pl.BlockSpec((1, 2, N_KEYS, KEY_DIM), lambda i, h: (h, 0, 0, 0))],
        out_specs=[spec] * 4,
        out_shape=[shape(BF16), shape(BF16), shape(F32), shape(F32)],
        compiler_params=_params(("parallel", "parallel")),
        name="router",
    )(q, keys)


EXPERT_SUB = 512
TILE16 = 16


def _experts_kernel(nsub, h2_ref, u_ref, vt_ref, r2_ref, e2_ref, n1_ref, e1_ref, x1_ref, gt2_ref, fg_ref, o_ref,
                    acc_ref):
    e = pl.program_id(1)
    na = EXPERT_SUB // N_KEYS
    tt = h2_ref.shape[0]

    @pl.when(e == 0)
    def _():
        acc_ref[...] = jnp.zeros_like(acc_ref)

    def first_dot(si):
        return _dot_nt(u_ref[si * EXPERT_SUB:(si + 1) * EXPERT_SUB, :], h2_ref[...])

    act = first_dot(0)
    total = None
    for si in range(nsub):
        nxt = first_dot(si + 1) if si + 1 < nsub else None
        gl = (0.5 * act * (1.0 + lax.erf(act * (2.0 ** -0.5)))).astype(BF16)
        parts = []
        for ai in range(na):
            a = (e * nsub + si) * na + ai
            gate = [jnp.zeros((TILE16, tt), BF16)] * (N_KEYS // TILE16)
            for h in range(PEER_HEADS):
                n1 = jnp.broadcast_to(n1_ref[h, pl.ds(a, 1), :], (TILE16, tt)).astype(BF16)
                e1 = jnp.broadcast_to(e1_ref[h, pl.ds(a, 1), :], (TILE16, tt)).astype(BF16)
                for rb in range(N_KEYS // TILE16):
                    rows = slice(rb * TILE16, (rb + 1) * TILE16)
                    zero = jnp.zeros((TILE16, tt), BF16)
                    gate[rb] = gate[rb] + jnp.where(r2_ref[h, rows, :] < n1, e2_ref[h, rows, :], zero) * e1
            parts.append(jnp.concatenate(gate, axis=0) * gl[ai * N_KEYS:(ai + 1) * N_KEYS])
        w = jnp.concatenate(parts, axis=0)
        part = _dot(vt_ref[:, si * EXPERT_SUB:(si + 1) * EXPERT_SUB], w)
        total = part if total is None else total + part
        act = nxt
    acc_ref[...] += total

    @pl.when(e == pl.num_programs(1) - 1)
    def _():
        x2 = x1_ref[...] + gt2_ref[...] * acc_ref[...].T
        y = x2 * lax.rsqrt(jnp.mean(x2 * x2, axis=-1, keepdims=True) + NORM_EPS)
        o_ref[...] = y * fg_ref[...]


def _experts(h2, u, vt, r2, e2, n1, e1, x1, gt2, fg, tt, et):
    t = h2.shape[0]
    ne = u.shape[0]
    tok = lambda i, e: (i, 0)
    fix = lambda i, e: (0, 0)
    rt = pl.BlockSpec((PEER_HEADS, N_KEYS, tt), lambda i, e: (0, 0, i))
    return pl.pallas_call(
        functools.partial(_experts_kernel, et // EXPERT_SUB),
        grid=(t // tt, ne // et),
        in_specs=[pl.BlockSpec((tt, D_MODEL), tok),
                  pl.BlockSpec((et, D_MODEL), lambda i, e: (e, 0)),
                  pl.BlockSpec((D_MODEL, et), lambda i, e: (0, e)),
                  rt, rt, rt, rt,
                  pl.BlockSpec((tt, D_MODEL), tok),
                  pl.BlockSpec((1, D_MODEL), fix), pl.BlockSpec((1, D_MODEL), fix)],
        out_specs=pl.BlockSpec((tt, D_MODEL), tok),
        out_shape=jax.ShapeDtypeStruct((t, D_MODEL), F32),
        scratch_shapes=[pltpu.VMEM((D_MODEL, tt), F32)],
        compiler_params=_params(("parallel", "arbitrary")),
        name="experts",
    )(h2, u, vt, r2, e2, n1, e1, x1, gt2, fg)


def _tile(n, pref):
    return pref if n % pref == 0 else n


def kernel(x, c, ctx, c_ctx, ada_w, ada_b, norm1_g, w_in, rwkv_conv, rwkv_w0, rwkv_w_up, rwkv_a0, rwkv_a_up, rwkv_g_up, rwkv_k_k, rwkv_k_a, rwkv_r_k, rwkv_ln_w, rwkv_ln_b, mlstm_conv, mlstm_i_b, mlstm_f_b, mlstm_norm_g, w_branch_a, w_branch_b, w_out, norm2_g, peer_wq, peer_keys, peer_u, peer_v, final_g):
    assert x.shape[0] == 1 and ada_w.shape[0] == 1, "one layer, batch 1"
    t, tc = x.shape[1], ctx.shape[1]
    rows = t // GRID_W
    xs, cs = x[0], ctx[0]

    cc = jnp.zeros((8, D_MODEL), F32).at[0].set(c[0]).at[1].set(c_ctx)
    mods = _ada(cc, ada_w[0], ada_b[0][None])
    sh1, sc1, gt1, sh2, sc2, gt2 = [mods[0:1, i * D_MODEL:(i + 1) * D_MODEL] for i in range(N_MOD)]
    csh1, csc1 = mods[1:2, 0:D_MODEL], mods[1:2, D_MODEL:2 * D_MODEL]

    w = w_in[0]
    o = 0
    parts = []
    for n in (RWKV_COLS, 2 * D_MLSTM_QK, D_MLSTM, 2 * N_DIR * MLSTM_HEADS, D_MLSTM, D_MODEL, D_MODEL):
        parts.append(w[:, o:o + n])
        o += n
    parts[3] = jnp.pad(parts[3], ((0, 0), (0, LANES - parts[3].shape[1])))
    weights = [p.astype(BF16) for p in parts]
    g1 = norm1_g[0][None]

    rw_prm = dict(conv=rwkv_conv[0], w0=rwkv_w0[0], w_up=rwkv_w_up[0].astype(BF16), a0=rwkv_a0[0],
                  a_up=rwkv_a_up[0].astype(BF16), k_k=rwkv_k_k[0][None], k_a=rwkv_k_a[0][None], r_k=rwkv_r_k[0][None],
                  ln_w=rwkv_ln_w[0][None], ln_b=rwkv_ln_b[0][None], g_up=rwkv_g_up[0].astype(BF16))
    bias = jnp.concatenate([mlstm_i_b[0].reshape(-1), mlstm_f_b[0].reshape(-1)])
    ml_prm = dict(conv=mlstm_conv[0], bias=jnp.pad(bias, (0, LANES - bias.shape[0]))[None], norm_g=mlstm_norm_g[0][None])
    nu = N_DIR * MLSTM_HEADS

    p_rw, p_qk, p_mv, p_if, _, _, _ = _proj(cs, g1, csh1, csc1, weights, _tile(tc, 256))
    s_zero = jnp.zeros((N_DIR, RWKV_PAIRS, LANES, LANES), F32)
    _, _, rw_state = _rwkv(p_rw, rw_prm, s_zero, tc, tc)
    m_zero = (jnp.zeros((nu, MLSTM_DQK, MLSTM_DV), F32), jnp.zeros((nu, MLSTM_DQK), F32), jnp.zeros((nu, LANES), F32))
    _, _, ml_state = _mlstm(p_qk, p_mv, p_if, ml_prm, m_zero, tc, 1)

    p_rw, p_qk, p_mv, p_if, p_o, p_ga, p_gb = _proj(xs, g1, sh1, sc1, weights, _tile(t, 256))
    ya0, ya1, _ = _rwkv(p_rw, rw_prm, rw_state, _tile(t, 256), GRID_W)
    hb0, hb1, _ = _mlstm(p_qk, p_mv, p_if, ml_prm, ml_state, rows, GRID_W)

    x1, h2, q = _merge(xs, ya0, ya1, hb0, hb1, p_o, p_ga, p_gb, w_branch_a[0].astype(BF16), w_branch_b[0].astype(BF16),
                       w_out[0].astype(BF16), gt1, norm2_g[0][None], sh2, sc2, peer_wq[0].astype(BF16), _tile(t, 256))
    r2, e2, n1, e1 = _router(q, peer_keys[0].astype(BF16), _tile(t, 512))
    out = _experts(h2, peer_u[0].astype(BF16), peer_v[0].astype(BF16).T, r2, e2, n1, e1, x1, gt2, final_g[None],
                   _tile(t, 512), 2048)
    return out[None]
```

```python
import functools

import jax
import jax.numpy as jnp
from jax import lax
from jax.experimental import pallas as pl
from jax.experimental.pallas import tpu as pltpu

F32 = jnp.float32
BF16 = jnp.bfloat16

D_MODEL = 1024
GRID_W = 64
N_MOD = 6
NORM_EPS = 1e-6

RWKV_HEAD = 64
RWKV_HEADS = 8
D_RWKV = RWKV_HEADS * RWKV_HEAD
LORA_W = 64
LORA_A = 64
LORA_G = 128
RWKV_COLS = 3 * D_RWKV + LORA_W + LORA_A + LORA_G
GN_EPS = 64e-5
RWKV_PAIRS = RWKV_HEADS // 2

MLSTM_HEADS = 4
MLSTM_DQK = 64
MLSTM_DV = 128
D_MLSTM_QK = MLSTM_HEADS * MLSTM_DQK
D_MLSTM = MLSTM_HEADS * MLSTM_DV
N_DIR = 2
CHUNK = 64

PEER_HEADS = 8
N_KEYS = 128
PEER_TOPK = 16
KEY_DIM = 128
PEER_QDIM = 2 * KEY_DIM

LANES = 128
VMEM_LIMIT = 56 * 1024 * 1024

_HI = lax.Precision.HIGHEST


def _dot(a, b, precision=None):
    return jnp.dot(a, b, preferred_element_type=F32, precision=precision)


def _dot_nt(a, b):
    return lax.dot_general(a, b, (((1,), (1,)), ((), ())), preferred_element_type=F32)


def _dot_tn(a, b):
    return lax.dot_general(a, b, (((0,), (0,)), ((), ())), preferred_element_type=F32)


def _sigmoid(x):
    return 1.0 / (1.0 + jnp.exp(-x))


def _params(sem):
    return pltpu.CompilerParams(dimension_semantics=sem, vmem_limit_bytes=VMEM_LIMIT)


def _scan_rows(x, reverse):
    n = x.shape[0]
    row = lax.broadcasted_iota(jnp.int32, x.shape, 0)
    d = 1
    while d < n:
        if reverse:
            x = x + jnp.where(row < n - d, pltpu.roll(x, n - d, axis=0), 0.0)
        else:
            x = x + jnp.where(row >= d, pltpu.roll(x, d, axis=0), 0.0)
        d *= 2
    return x


def _conv3_rows(x, w, period):
    n = x.shape[0]
    pos = lax.broadcasted_iota(jnp.int32, x.shape, 0) % period
    prev = jnp.where(pos == 0, 0.0, pltpu.roll(x, 1, axis=0))
    nxt = jnp.where(pos == period - 1, 0.0, pltpu.roll(x, n - 1, axis=0))
    return w[0:1] * prev + w[1:2] * x + w[2:3] * nxt


def _ada_kernel(c_ref, w_ref, b_ref, o_ref):
    c = c_ref[...]
    s = c * _sigmoid(c)
    o_ref[...] = _dot(s, w_ref[...], precision=_HI) + b_ref[...]


def _ada(cc, w, b):
    n = w.shape[1]
    bn = n // 4
    return pl.pallas_call(
        _ada_kernel,
        grid=(n // bn,),
        in_specs=[pl.BlockSpec((8, D_MODEL), lambda i: (0, 0)),
                  pl.BlockSpec((D_MODEL, bn), lambda i: (0, i)),
                  pl.BlockSpec((1, bn), lambda i: (0, i))],
        out_specs=pl.BlockSpec((8, bn), lambda i: (0, i)),
        out_shape=jax.ShapeDtypeStruct((8, n), F32),
        compiler_params=_params(("arbitrary",)),
        name="ada",
    )(cc, w, b)


def _proj_kernel(nw, x_ref, g_ref, sh_ref, sc_ref, *refs):
    x = x_ref[...]
    y = x * lax.rsqrt(jnp.mean(x * x, axis=-1, keepdims=True) + NORM_EPS)
    h = (y * g_ref[...]) * (1.0 + sc_ref[...]) + sh_ref[...]
    hb = h.astype(BF16)
    for w_ref, o_ref in zip(refs[:nw], refs[nw:]):
        o_ref[...] = _dot(hb, w_ref[...])


def _proj(x, g, shift, scale, weights, tm):
    t = x.shape[0]
    row = lambda i: (i, 0)
    fix = lambda i: (0, 0)
    in_specs = [pl.BlockSpec((tm, D_MODEL), row)] + [pl.BlockSpec((1, D_MODEL), fix)] * 3
    in_specs += [pl.BlockSpec(w.shape, fix) for w in weights]
    return pl.pallas_call(
        functools.partial(_proj_kernel, len(weights)),
        grid=(t // tm,),
        in_specs=in_specs,
        out_specs=[pl.BlockSpec((tm, w.shape[1]), row) for w in weights],
        out_shape=[jax.ShapeDtypeStruct((t, w.shape[1]), F32) for w in weights],
        compiler_params=_params(("parallel",)),
        name="proj",
    )(x, g, shift, scale, *weights)


def _rwkv_prep(z, f_ref, conv_ref, w0_ref, wup_ref, a0_ref, aup_ref, kk_ref, ka_ref, gup_ref, period, dst):
    f = _conv3_rows(f_ref[...], conv_ref[...], period)
    r = f[:, 0:D_RWKV]
    k = f[:, D_RWKV:2 * D_RWKV]
    v = f[:, 2 * D_RWKV:3 * D_RWKV]
    o = 3 * D_RWKV
    wd = f[:, o:o + LORA_W]
    ad = f[:, o + LORA_W:o + LORA_W + LORA_A]
    gd = f[:, o + LORA_W + LORA_A:o + LORA_W + LORA_A + LORA_G]
    lw = w0_ref[z:z + 1, :] + _dot(jnp.tanh(wd).astype(BF16), wup_ref[z])
    w_log = -(jnp.maximum(-lw, 0.0) + jnp.log(1.0 + jnp.exp(-jnp.abs(lw)))) - 0.5
    a = _sigmoid(a0_ref[z:z + 1, :] + _dot(ad.astype(BF16), aup_ref[z]))
    g = _dot(_sigmoid(gd).astype(BF16), gup_ref[...])
    logw_s, kraw_s, a_s, keff_s, v_s, r_s, g_s = dst
    logw_s[z] = -jnp.exp(w_log)
    kraw_s[z] = k * kk_ref[...]
    a_s[z] = a
    keff_s[z] = k * (1.0 + (a - 1.0) * ka_ref[...])
    v_s[z] = v
    r_s[z] = r
    g_s[z] = g


def _rwkv_setup(units, masks):
    def st(x, lane_lo):
        return jnp.concatenate([jnp.where(lane_lo, x, 0.0), jnp.where(lane_lo, 0.0, x)], axis=0)

    def mm(p, q):
        return _dot(p.astype(BF16), q.astype(BF16))

    def each(f, *lists):
        return [f(*args) for args in zip(*lists)]

    pre = []
    for z, lw, kraw, a, keff, v, r, rk in units:
        lane_lo = masks[z][0]
        rev = z == 1
        cum = _scan_rows(lw, rev)
        tot = cum[0:1, :] if rev else cum[CHUNK - 1:CHUNK, :]
        e_in = jnp.exp(cum)
        e_ex = jnp.exp(cum - lw)
        e_ng = jnp.exp(-cum)
        e_rem = jnp.exp(tot - cum)
        kr = st(kraw, lane_lo)
        inv = 1.0 / jnp.maximum(jnp.sqrt(jnp.sum(kr * kr, axis=1, keepdims=True)), 1e-12)
        ka = kraw * a
        vs = st(v, lane_lo)
        pre.append(dict(
            z=z, xk=(st(kraw * e_ex, lane_lo) * inv).astype(BF16), xr=st(r * e_in, lane_lo).astype(BF16),
            yk=st(keff * e_ng, lane_lo).astype(BF16), yb=(st(ka * e_ng, lane_lo) * inv).astype(BF16),
            ykg=st(keff * e_rem, lane_lo).astype(BF16), ybg=(st(ka * e_rem, lane_lo) * inv).astype(BF16),
            vs=vs, vb=vs.astype(BF16), gam=jnp.exp(tot),
            bonus=jnp.sum(st(r * keff * rk, lane_lo), axis=1, keepdims=True)))
    zs = [p["z"] for p in pre]

    m = [_dot_nt(jnp.concatenate([p["xk"], p["xr"]], axis=0), jnp.concatenate([p["yk"], p["yb"]], axis=0)) for p in pre]
    akk = [jnp.where(masks[z][2], x[:LANES, :LANES], 0.0).astype(BF16) for z, x in zip(zs, m)]
    ark = [jnp.where(masks[z][3], x[LANES:, :LANES], 0.0).astype(BF16) for z, x in zip(zs, m)]
    arb = [jnp.where(masks[z][3], x[LANES:, LANES:], 0.0).astype(BF16) for z, x in zip(zs, m)]
    a_d = [jnp.where(masks[z][2] & masks[z][4], x[:LANES, LANES:], 0.0) for z, x in zip(zs, m)]
    a_off = [jnp.where(masks[z][2] & jnp.logical_not(masks[z][4]), x[:LANES, LANES:], 0.0) for z, x in zip(zs, m)]
    eye = masks[0][5]

    a2 = each(mm, a_d, a_d)
    t_d = [eye - x for x in a_d]
    avk = [_dot(k, p["vb"]) for k, p in zip(akk, pre)]
    a4 = each(mm, a2, a2)
    t_d = each(lambda t, x: t + mm(t, x), t_d, a2)
    avr = [_dot(k, p["vb"]) for k, p in zip(ark, pre)]
    a8 = each(mm, a4, a4)
    t_d = each(lambda t, x: t + mm(t, x), t_d, a4)
    ds0 = [_dot_tn(p["vb"], p["ykg"]) for p in pre]
    t_d = each(lambda t, x: t + mm(t, x), t_d, a8)
    n1 = each(mm, t_d, a_off)
    n2 = each(mm, n1, n1)
    n3 = each(mm, n1, n2)
    tmat = each(lambda x1, x2, x3, t: mm(eye - x1 + x2 - x3, t).astype(BF16), n1, n2, n3, t_d)
    return [dict(z=p["z"], xk=p["xk"], xr=p["xr"], ybg=p["ybg"], vs=p["vs"], gam=p["gam"], bonus=p["bonus"],
                 tmat=t, avk=k, avr=r, arb=b, ds0=d)
            for p, t, k, r, b, d in zip(pre, tmat, avk, avr, arb, ds0)]


def _rwkv_advance(pre, s_prev, lnw, lnb, masks):
    sb = [s.astype(BF16) for s in s_prev]
    rhs = [_dot_nt(p["xk"], s) + p["avk"] for p, s in zip(pre, sb)]
    osr = [_dot_nt(p["xr"], s) + p["avr"] for p, s in zip(pre, sb)]
    ub = [_dot(p["tmat"], x.astype(BF16)).astype(BF16) for p, x in zip(pre, rhs)]
    o = [x - _dot(p["arb"], u) for x, p, u in zip(osr, pre, ub)]
    s_new = [s * p["gam"] + p["ds0"] - _dot_tn(u, p["ybg"]) for s, p, u in zip(s_prev, pre, ub)]
    ys = []
    for p, x, w, b in zip(pre, o, lnw, lnb):
        own = masks[p["z"]][1]
        mu = jnp.sum(x, axis=1, keepdims=True) * (1.0 / RWKV_HEAD)
        cen = jnp.where(own, x - mu, 0.0)
        var = jnp.sum(cen * cen, axis=1, keepdims=True) * (1.0 / RWKV_HEAD)
        y = cen * lax.rsqrt(var + GN_EPS) * w + jnp.where(own, b, 0.0) + p["bonus"] * p["vs"]
        ys.append(y[:CHUNK] + y[CHUNK:])
    return ys, s_new


def _rwkv_masks():
    i = lax.broadcasted_iota(jnp.int32, (LANES, LANES), 0)
    j = lax.broadcasted_iota(jnp.int32, (LANES, LANES), 1)
    lane_lo = lax.broadcasted_iota(jnp.int32, (CHUNK, LANES), 1) < RWKV_HEAD
    same = (i // CHUNK) == (j // CHUNK)
    eye = jnp.where(i == j, 1.0, 0.0).astype(F32)
    diag16 = (i // 16) == (j // 16)
    out = []
    for z in range(N_DIR):
        before = (j % CHUNK) > (i % CHUNK) if z == 1 else (j % CHUNK) < (i % CHUNK)
        strict = same & before
        incl = same & (before | (i == j))
        out.append((lane_lo, same, strict, incl, diag16, eye))
    return out


def _rwkv_kernel(period, nch, f0_ref, f1_ref, conv_ref, w0_ref, wup_ref, a0_ref, aup_ref, kk_ref, ka_ref, rk_ref,
                 lnw_ref, lnb_ref, gup_ref, sin_ref, y0_ref, y1_ref, sout_ref,
                 s_ref, logw_s, kraw_s, a_s, keff_s, v_s, r_s, g_s):
    step = pl.program_id(0)

    @pl.when(step == 0)
    def _():
        s_ref[...] = sin_ref[...]

    dst = (logw_s, kraw_s, a_s, keff_s, v_s, r_s, g_s)
    for z, f_ref in ((0, f0_ref), (1, f1_ref)):
        _rwkv_prep(z, f_ref, conv_ref, w0_ref, wup_ref, a0_ref, aup_ref, kk_ref, ka_ref, gup_ref, period, dst)

    masks = _rwkv_masks()
    y_refs = (y0_ref, y1_ref)

    units, where = [], []
    for ci in range(nch):
        for z in range(N_DIR):
            r0 = (nch - 1 - ci if z == 1 else ci) * CHUNK
            rows = slice(r0, r0 + CHUNK)
            for p in range(RWKV_PAIRS):
                ls = slice(p * LANES, (p + 1) * LANES)
                units.append((z, logw_s[z, rows, ls], kraw_s[z, rows, ls], a_s[z, rows, ls], keff_s[z, rows, ls],
                              v_s[z, rows, ls], r_s[z, rows, ls], rk_ref[:, ls]))
                where.append((z, p, rows, ls))
    pre = _rwkv_setup(units, masks)
    per = N_DIR * RWKV_PAIRS
    state = [s_ref[z, p] for z in range(N_DIR) for p in range(RWKV_PAIRS)]
    lnw = [lnw_ref[:, p * LANES:(p + 1) * LANES] for z in range(N_DIR) for p in range(RWKV_PAIRS)]
    lnb = [lnb_ref[:, p * LANES:(p + 1) * LANES] for z in range(N_DIR) for p in range(RWKV_PAIRS)]
    for ci in range(nch):
        ys, state = _rwkv_advance(pre[ci * per:(ci + 1) * per], state, lnw, lnb, masks)
        for (z, p, rows, ls), y in zip(where[ci * per:(ci + 1) * per], ys):
            y_refs[z][rows, ls] = y * g_s[z, rows, ls]
    for i, s_new in enumerate(state):
        s_ref[i // RWKV_PAIRS, i % RWKV_PAIRS] = s_new

    @pl.when(step == pl.num_programs(0) - 1)
    def _():
        sout_ref[...] = s_ref[...]


def _rwkv(feat, prm, s_init, tb, period):
    t = feat.shape[0]
    nb = t // tb
    nch = tb // CHUNK
    fix2 = lambda i: (0, 0)
    fix3 = lambda i: (0, 0, 0)
    fix4 = lambda i: (0, 0, 0, 0)
    fwd = lambda i: (i, 0)
    bwd = lambda i: (nb - 1 - i, 0)
    vec = pl.BlockSpec((1, D_RWKV), fix2)
    state = pl.BlockSpec((N_DIR, RWKV_PAIRS, LANES, LANES), fix4)
    big = pltpu.VMEM((N_DIR, tb, D_RWKV), F32)
    return pl.pallas_call(
        functools.partial(_rwkv_kernel, period, nch),
        grid=(nb,),
        in_specs=[pl.BlockSpec((tb, RWKV_COLS), fwd), pl.BlockSpec((tb, RWKV_COLS), bwd),
                  pl.BlockSpec((3, RWKV_COLS), fix2),
                  pl.BlockSpec((N_DIR, D_RWKV), fix2), pl.BlockSpec((N_DIR, LORA_W, D_RWKV), fix3),
                  pl.BlockSpec((N_DIR, D_RWKV), fix2), pl.BlockSpec((N_DIR, LORA_A, D_RWKV), fix3),
                  vec, vec, vec, vec, vec,
                  pl.BlockSpec((LORA_G, D_RWKV), fix2), state],
        out_specs=[pl.BlockSpec((tb, D_RWKV), fwd), pl.BlockSpec((tb, D_RWKV), bwd), state],
        out_shape=[jax.ShapeDtypeStruct((t, D_RWKV), F32), jax.ShapeDtypeStruct((t, D_RWKV), F32),
                   jax.ShapeDtypeStruct((N_DIR, RWKV_PAIRS, LANES, LANES), F32)],
        scratch_shapes=[pltpu.VMEM((N_DIR, RWKV_PAIRS, LANES, LANES), F32)] + [big] * 7,
        compiler_params=_params(("arbitrary",)),
        name="rwkv",
    )(feat, feat, prm["conv"], prm["w0"], prm["w_up"], prm["a0"], prm["a_up"], prm["k_k"], prm["k_a"], prm["r_k"],
      prm["ln_w"], prm["ln_b"], prm["g_up"], s_init)


def _cummax_rows(x, reverse):
    n = x.shape[0]
    row = lax.broadcasted_iota(jnp.int32, x.shape, 0)
    d = 1
    while d < n:
        if reverse:
            x = jnp.maximum(x, jnp.where(row < n - d, pltpu.roll(x, n - d, axis=0), -jnp.inf))
        else:
            x = jnp.maximum(x, jnp.where(row >= d, pltpu.roll(x, d, axis=0), -jnp.inf))
        d *= 2
    return x


def _mlstm_kernel(nch, qk0_ref, qk1_ref, mv0_ref, mv1_ref, if0_ref, if1_ref, conv_ref, bias_ref, ng_ref,
                  cin_ref, min_ref, h0_ref, h1_ref, cout_ref, mout_ref,
                  c_s, m_s, qk_s, gi_s, lf_s):
    step = pl.program_id(0)
    rows_n = qk0_ref.shape[0]
    nh = MLSTM_HEADS

    @pl.when(step == 0)
    def _():
        c_s[...] = cin_ref[...]
        m_s[...] = min_ref[...]

    for z, (qk_ref, if_ref) in enumerate(((qk0_ref, if0_ref), (qk1_ref, if1_ref))):
        qk = _conv3_rows(qk_ref[...], conv_ref[...], rows_n)
        qk_s[z] = qk * _sigmoid(qk)
        gate = if_ref[...] + bias_ref[...]
        gi_s[z] = gate[:, :LANES]
        fg = gate[:, LANES:]
        lf_s[z] = jnp.minimum(fg, 0.0) - jnp.log(1.0 + jnp.exp(-jnp.abs(fg)))

    ti = lax.broadcasted_iota(jnp.int32, (CHUNK, CHUNK), 0)
    si = lax.broadcasted_iota(jnp.int32, (CHUNK, CHUNK), 1)
    causal = (si <= ti, si >= ti)
    lane = lax.broadcasted_iota(jnp.int32, (1, LANES), 1)
    mv_refs = (mv0_ref, mv1_ref)
    h_refs = (h0_ref, h1_ref)
    ones = jnp.ones((CHUNK, LANES), BF16)
    lane0 = lax.broadcasted_iota(jnp.int32, (CHUNK, LANES), 1) == 0

    def chunk_body(ci, carry):
        mrow = m_s[0:1, :]
        us, m_next = [], mrow
        for z in range(N_DIR):
            rev = z == 1
            r0 = pl.multiple_of((nch - 1 - ci if rev else ci) * CHUNK, CHUNK)
            rows = pl.ds(r0, CHUNK)
            gi = gi_s[z, rows, :]
            b = _scan_rows(lf_s[z, rows, :], rev)
            d = gi - b
            bend = b[0:1, :] if rev else b[CHUNK - 1:CHUNK, :]
            m_row = b + jnp.maximum(mrow, _cummax_rows(d, rev))
            a_int = jnp.exp(b + mrow - m_row)
            c1 = b - m_row
            g_end = bend - b + gi
            m_new = jnp.maximum(bend + mrow, jnp.max(g_end, axis=0, keepdims=True))
            w_end = jnp.exp(g_end - m_new)
            keep = jnp.exp(bend + mrow - m_new)
            unit_lanes = (lane >= z * nh) & (lane < (z + 1) * nh)
            m_next = jnp.where(unit_lanes, m_new, m_next)
            d_t = d.T
            for h in range(nh):
                u = z * nh + h
                us.append(dict(
                    z=z, h=h, u=u, rows=rows, c1=c1[:, u:u + 1], drow=d_t[u:u + 1, :], a_int=a_int[:, u:u + 1],
                    w_end=w_end[:, u:u + 1], keep=keep[:, u:u + 1], einv=jnp.exp(-m_row[:, u:u + 1]),
                    q=qk_s[z, rows, h * MLSTM_DQK:(h + 1) * MLSTM_DQK],
                    k=qk_s[z, rows, D_MLSTM_QK + h * MLSTM_DQK:D_MLSTM_QK + (h + 1) * MLSTM_DQK] * (MLSTM_DQK ** -0.5),
                    v=mv_refs[z][rows, h * MLSTM_DV:(h + 1) * MLSTM_DV], ct=c_s[u]))
        qb = [x["q"].astype(BF16) for x in us]
        kb = [x["k"].astype(BF16) for x in us]
        qk = [_dot_nt(x, y) for x, y in zip(qb, kb)]
        qc = [_dot(x, y["ct"].astype(BF16)) for x, y in zip(qb, us)]
        dc = [_dot_tn(x, jnp.concatenate([(y["v"] * y["w_end"]).astype(BF16),
                                          jnp.where(lane0, y["w_end"], 0.0).astype(BF16)], axis=1))
              for x, y in zip(kb, us)]
        smat = [x * jnp.exp(jnp.where(causal[y["z"]], y["c1"] + y["drow"], -jnp.inf)) for x, y in zip(qk, us)]
        sv = [_dot(x.astype(BF16), jnp.concatenate([y["v"].astype(BF16), ones], axis=1)) for x, y in zip(smat, us)]
        hh = [(y[:, :MLSTM_DV] + x["a_int"] * w[:, :MLSTM_DV])
              / jnp.maximum(jnp.abs(y[:, MLSTM_DV:MLSTM_DV + 1] + x["a_int"] * w[:, MLSTM_DV:MLSTM_DV + 1]), x["einv"])
              for x, y, w in zip(us, sv, qc)]
        hms = [jnp.mean(x * x, axis=1, keepdims=True) for x in hh]
        for x, y, ms, dcu in zip(us, hh, hms, dc):
            u, h = x["u"], x["h"]
            cols = slice(h * MLSTM_DV, (h + 1) * MLSTM_DV)
            h_refs[x["z"]][x["rows"], cols] = y * lax.rsqrt(ms + NORM_EPS) * ng_ref[:, cols]
            c_s[u] = x["keep"] * x["ct"] + dcu
        m_s[...] = jnp.broadcast_to(m_next, m_s.shape)
        return carry

    lax.fori_loop(0, nch, chunk_body, 0)

    @pl.when(step == pl.num_programs(0) - 1)
    def _():
        cout_ref[...] = c_s[...]
        mout_ref[...] = m_s[...]


def _mlstm(qk, mv, gates, prm, state, rows, ncol):
    nch = rows // CHUNK
    nu = N_DIR * MLSTM_HEADS
    qk2 = qk.reshape(rows, ncol * 2 * D_MLSTM_QK)
    mv2 = mv.reshape(rows, ncol * D_MLSTM)
    g2 = gates.reshape(rows, ncol * 2 * LANES)
    fwd = lambda i: (0, i)
    bwd = lambda i: (0, ncol - 1 - i)
    fix2 = lambda i: (0, 0)
    fix3 = lambda i: (0, 0, 0)
    cspec = pl.BlockSpec((nu, MLSTM_DQK, 2 * MLSTM_DV), fix3)
    mspec = pl.BlockSpec((8, LANES), fix2)
    wide = lambda m: pl.BlockSpec((rows, 2 * D_MLSTM_QK), m)
    outs = pl.pallas_call(
        functools.partial(_mlstm_kernel, nch),
        grid=(ncol,),
        in_specs=[wide(fwd), wide(bwd), wide(fwd), wide(bwd),
                  pl.BlockSpec((rows, 2 * LANES), fwd), pl.BlockSpec((rows, 2 * LANES), bwd),
                  pl.BlockSpec((3, 2 * D_MLSTM_QK), fix2), pl.BlockSpec((1, 2 * LANES), fix2),
                  pl.BlockSpec((1, D_MLSTM), fix2), cspec, mspec],
        out_specs=[wide(fwd), wide(bwd), cspec, mspec],
        out_shape=[jax.ShapeDtypeStruct((rows, ncol * D_MLSTM), F32)] * 2
        + [jax.ShapeDtypeStruct((nu, MLSTM_DQK, 2 * MLSTM_DV), F32), jax.ShapeDtypeStruct((8, LANES), F32)],
        scratch_shapes=[pltpu.VMEM((nu, MLSTM_DQK, 2 * MLSTM_DV), F32), pltpu.VMEM((8, LANES), F32),
                        pltpu.VMEM((N_DIR, rows, 2 * D_MLSTM_QK), F32),
                        pltpu.VMEM((N_DIR, rows, LANES), F32), pltpu.VMEM((N_DIR, rows, LANES), F32)],
        compiler_params=_params(("arbitrary",)),
        name="mlstm",
    )(qk2, qk2, mv2, mv2, g2, g2, prm["conv"], prm["bias"], prm["norm_g"], *state)
    h0, h1 = outs[0].reshape(rows * ncol, D_MLSTM), outs[1].reshape(rows * ncol, D_MLSTM)
    return h0, h1, tuple(outs[2:])


def _merge_kernel(x_ref, ya0_ref, ya1_ref, hb0_ref, hb1_ref, po_ref, ga_ref, gb_ref, wa_ref, wb_ref, wo_ref,
                  gt1_ref, g2_ref, sh2_ref, sc2_ref, wq_ref, x1_ref, h2_ref, q_ref):
    ya = ya0_ref[...] + ya1_ref[...]
    yb = (hb0_ref[...] + hb1_ref[...]) * _sigmoid(po_ref[...])
    merged = (_sigmoid(ga_ref[...]) * _dot(ya.astype(BF16), wa_ref[...])
              + _sigmoid(gb_ref[...]) * _dot(yb.astype(BF16), wb_ref[...]))
    x1 = x_ref[...] + gt1_ref[...] * _dot(merged.astype(BF16), wo_ref[...])
    x1_ref[...] = x1
    y = x1 * lax.rsqrt(jnp.mean(x1 * x1, axis=-1, keepdims=True) + NORM_EPS)
    h2 = ((y * g2_ref[...]) * (1.0 + sc2_ref[...]) + sh2_ref[...]).astype(BF16)
    h2_ref[...] = h2
    q_ref[...] = _dot(h2, wq_ref[...]).astype(q_ref.dtype)


def _merge(x, ya0, ya1, hb0, hb1, po, ga, gb, wa, wb, wo, gt1, g2, sh2, sc2, wq, tm):
    t = x.shape[0]
    row = lambda i: (i, 0)
    fix = lambda i: (0, 0)
    rs = lambda n: pl.BlockSpec((tm, n), row)
    vec = pl.BlockSpec((1, D_MODEL), fix)
    nq = wq.shape[1]
    return pl.pallas_call(
        _merge_kernel,
        grid=(t // tm,),
        in_specs=[rs(D_MODEL), rs(D_RWKV), rs(D_RWKV), rs(D_MLSTM), rs(D_MLSTM), rs(D_MLSTM), rs(D_MODEL), rs(D_MODEL),
                  pl.BlockSpec(wa.shape, fix), pl.BlockSpec(wb.shape, fix), pl.BlockSpec(wo.shape, fix),
                  vec, vec, vec, vec, pl.BlockSpec(wq.shape, fix)],
        out_specs=[rs(D_MODEL), rs(D_MODEL), rs(nq)],
        out_shape=[jax.ShapeDtypeStruct((t, D_MODEL), F32), jax.ShapeDtypeStruct((t, D_MODEL), BF16),
                   jax.ShapeDtypeStruct((t, nq), BF16)],
        compiler_params=_params(("parallel",)),
        name="merge",
    )(x, ya0, ya1, hb0, hb1, po, ga, gb, wa, wb, wo, gt1, g2, sh2, sc2, wq)


def _sort_pairs(n):
    pairs = []
    t = max(1, (n - 1).bit_length())
    p = 1 << (t - 1)
    while p > 0:
        q, r, d = 1 << (t - 1), 0, p
        while d > 0:
            pairs += [(i, i + d) for i in range(n - d) if (i & p) == r]
            d, q, r = q - p, q >> 1, p
        p >>= 1
    return pairs


def _bitonic_desc(c):
    n = len(c)
    d = n // 2
    while d > 0:
        for i in range(n):
            if (i & d) == 0:
                c[i], c[i + d] = jnp.maximum(c[i], c[i + d]), jnp.minimum(c[i], c[i + d])
        d //= 2
    return c


def _merge_top(x, y):
    n = len(x)
    return _bitonic_desc([jnp.maximum(x[i], y[n - 1 - i]) for i in range(n)])


def _top16_levels(scores):
    x = [scores[8 * i:8 * (i + 1), :] for i in range(N_KEYS // 8)]
    for i, j in _sort_pairs(len(x)):
        x[i], x[j] = jnp.maximum(x[i], x[j]), jnp.minimum(x[i], x[j])
    for shift in (4, 2, 1):
        x = _merge_top(x, [pltpu.roll(v, shift, axis=0) for v in x])
    return x


def _router_kernel(q_ref, keys_ref, r2_ref, e2_ref, n1_ref, e1_ref):
    qb = q_ref[...]
    s1 = _dot_nt(keys_ref[0, 0], qb[:, :KEY_DIM])
    s2 = _dot_nt(keys_ref[0, 1], qb[:, KEY_DIM:])
    a = _top16_levels(s1)
    b = _top16_levels(s2)
    k = PEER_TOPK
    nj = [k // (i + 1) for i in range(k)]
    cell = {(i, j): a[i] + b[j] for i in range(k) for j in range(nj[i])}
    ninf = jnp.full(a[0].shape, -jnp.inf, F32)
    pad = lambda lst: lst + [ninf] * (k - len(lst))
    lists = [pad([cell[(i, j)] for j in range(nj[i])]) for i in range(4)]
    lists += [pad([cell[(i, j)] for i in range(4, k) if j < nj[i]]) for j in range(3)]
    top = lists[0]
    for other in lists[1:]:
        top = _merge_top(top, other)
    tau = top[k - 1]
    mx = cell[(0, 0)]
    zsum = jnp.zeros_like(tau)
    n_rank = []
    for i in range(k):
        cnt = jnp.zeros_like(tau)
        for j in range(nj[i]):
            sel = cell[(i, j)] >= tau
            zsum = zsum + jnp.where(sel, jnp.exp(cell[(i, j)] - mx), 0.0)
            cnt = cnt + jnp.where(sel, 1.0, 0.0)
        n_rank.append(cnt)
    rz = 1.0 / zsum
    for blk in range(N_KEYS // 16):
        r2, e2 = [], []
        for half in range(2):
            rows = slice(16 * blk + 8 * half, 16 * blk + 8 * (half + 1))
            s1b, s2b = s1[rows, :], s2[rows, :]
            n1 = jnp.zeros_like(s1b)
            for i in reversed(range(k)):
                n1 = jnp.where(s1b == a[i], n_rank[i], n1)
            n1_ref[0, rows, :] = n1
            e1_ref[0, rows, :] = jnp.exp(s1b - a[0]) * rz
            rank = jnp.zeros_like(s2b)
            for lvl in b:
                rank = rank + jnp.where(lvl > s2b, 1.0, 0.0)
            r2.append(rank)
            e2.append(jnp.exp(s2b - b[0]))
        rows = slice(16 * blk, 16 * (blk + 1))
        r2_ref[0, rows, :] = jnp.concatenate(r2, axis=0).astype(BF16)
        e2_ref[0, rows, :] = jnp.concatenate(e2, axis=0).astype(BF16)


def _router(q, keys, tt):
    t = q.shape[0]
    spec = pl.BlockSpec((1, N_KEYS, tt), lambda i, h: (h, 0, i))
    shape = lambda dt: jax.ShapeDtypeStruct((PEER_HEADS, N_KEYS, t), dt)
    return pl.pallas_call(
        _router_kernel,
        grid=(t // tt, PEER_HEADS),
        in_specs=[pl.BlockSpec((tt, PEER_QDIM), lambda i, h: (i, h)),
                  pl.BlockSpec((1, 2, N_KEYS, KEY_DIM), lambda i, h: (h, 0, 0, 0))],
        out_specs=[spec] * 4,
        out_shape=[shape(BF16), shape(BF16), shape(F32), shape(F32)],
        compiler_params=_params(("parallel", "parallel")),
        name="router",
    )(q, keys)


EXPERT_SUB = 512
TILE16 = 16
GATE_TOKENS = 512


def _experts_kernel(nsub, h2_ref, u_ref, vt_ref, r2_ref, e2_ref, n1_ref, e1_ref, x1_ref, gt2_ref, fg_ref, o_ref,
                    acc_ref, act_ref, w_ref):
    e = pl.program_id(1)
    na = EXPERT_SUB // N_KEYS
    tt = h2_ref.shape[0]

    @pl.when(e == 0)
    def _():
        acc_ref[...] = jnp.zeros_like(acc_ref)

    def first_dot(si, dst):
        act_ref[dst] = _dot_nt(u_ref[si * EXPERT_SUB:(si + 1) * EXPERT_SUB, :], h2_ref[...])

    def build(si, cur):
        for ai in range(na):
            a = si * na + ai
            rows_a = slice(ai * N_KEYS, (ai + 1) * N_KEYS)
            for tb in range(tt // GATE_TOKENS):
                tok = slice(tb * GATE_TOKENS, (tb + 1) * GATE_TOKENS)
                gate = [jnp.zeros((TILE16, GATE_TOKENS), BF16)] * (N_KEYS // TILE16)
                for h in range(PEER_HEADS):
                    n1 = jnp.broadcast_to(n1_ref[h, a:a + 1, tok], (TILE16, GATE_TOKENS)).astype(BF16)
                    e1 = jnp.broadcast_to(e1_ref[h, a:a + 1, tok], (TILE16, GATE_TOKENS)).astype(BF16)
                    for rb in range(N_KEYS // TILE16):
                        rows = slice(rb * TILE16, (rb + 1) * TILE16)
                        zero = jnp.zeros((TILE16, GATE_TOKENS), BF16)
                        gate[rb] = gate[rb] + jnp.where(r2_ref[h, rows, tok] < n1, e2_ref[h, rows, tok], zero) * e1
                act = act_ref[cur, rows_a, tok]
                gl = (0.5 * act * (1.0 + lax.erf(act * (2.0 ** -0.5)))).astype(BF16)
                w_ref[cur, rows_a, tok] = jnp.concatenate(gate, axis=0) * gl

    first_dot(0, 0)
    total = None
    for si in range(nsub):
        cur = si % 2
        if si + 1 < nsub:
            first_dot(si + 1, 1 - cur)
        build(si, cur)
        part = _dot(vt_ref[si], w_ref[cur])
        total = part if total is None else total + part
    acc_ref[...] += total

    @pl.when(e == pl.num_programs(1) - 1)
    def _():
        x2 = x1_ref[...] + gt2_ref[...] * acc_ref[...].T
        y = x2 * lax.rsqrt(jnp.mean(x2 * x2, axis=-1, keepdims=True) + NORM_EPS)
        o_ref[...] = y * fg_ref[...]


def _experts(h2, u, vt, r2, e2, n1, e1, x1, gt2, fg, tt, et):
    t = h2.shape[0]
    ne = u.shape[0]
    tok = lambda i, e: (i, 0)
    fix = lambda i, e: (0, 0)
    rt = pl.BlockSpec((PEER_HEADS, N_KEYS, tt), lambda i, e: (0, 0, i))
    rs = pl.BlockSpec((PEER_HEADS, et // N_KEYS, tt), lambda i, e: (0, e, i))
    return pl.pallas_call(
        functools.partial(_experts_kernel, et // EXPERT_SUB),
        grid=(t // tt, ne // et),
        in_specs=[pl.BlockSpec((tt, D_MODEL), tok),
                  pl.BlockSpec((et, D_MODEL), lambda i, e: (e, 0)),
                  pl.BlockSpec((et // EXPERT_SUB, D_MODEL, EXPERT_SUB), lambda i, e: (e, 0, 0)),
                  rt, rt, rs, rs,
                  pl.BlockSpec((tt, D_MODEL), tok),
                  pl.BlockSpec((1, D_MODEL), fix), pl.BlockSpec((1, D_MODEL), fix)],
        out_specs=pl.BlockSpec((tt, D_MODEL), tok),
        out_shape=jax.ShapeDtypeStruct((t, D_MODEL), F32),
        scratch_shapes=[pltpu.VMEM((D_MODEL, tt), F32), pltpu.VMEM((2, EXPERT_SUB, tt), F32),
                        pltpu.VMEM((2, EXPERT_SUB, tt), BF16)],
        compiler_params=_params(("parallel", "arbitrary")),
        name="experts",
    )(h2, u, vt, r2, e2, n1, e1, x1, gt2, fg)


def _tile(n, pref):
    return pref if n % pref == 0 else n


def kernel(x, c, ctx, c_ctx, ada_w, ada_b, norm1_g, w_in, rwkv_conv, rwkv_w0, rwkv_w_up, rwkv_a0, rwkv_a_up, rwkv_g_up, rwkv_k_k, rwkv_k_a, rwkv_r_k, rwkv_ln_w, rwkv_ln_b, mlstm_conv, mlstm_i_b, mlstm_f_b, mlstm_norm_g, w_branch_a, w_branch_b, w_out, norm2_g, peer_wq, peer_keys, peer_u, peer_v, final_g):
    assert x.shape[0] == 1 and ada_w.shape[0] == 1, "one layer, batch 1"
    t, tc = x.shape[1], ctx.shape[1]
    rows = t // GRID_W
    xs, cs = x[0], ctx[0]

    cc = jnp.zeros((8, D_MODEL), F32).at[0].set(c[0]).at[1].set(c_ctx)
    mods = _ada(cc, ada_w[0], ada_b[0][None])
    sh1, sc1, gt1, sh2, sc2, gt2 = [mods[0:1, i * D_MODEL:(i + 1) * D_MODEL] for i in range(N_MOD)]
    csh1, csc1 = mods[1:2, 0:D_MODEL], mods[1:2, D_MODEL:2 * D_MODEL]

    w = w_in[0]
    o = 0
    parts = []
    for n in (RWKV_COLS, 2 * D_MLSTM_QK, D_MLSTM, 2 * N_DIR * MLSTM_HEADS, D_MLSTM, D_MODEL, D_MODEL):
        parts.append(w[:, o:o + n])
        o += n
    ng = N_DIR * MLSTM_HEADS
    gpad = ((0, 0), (0, LANES - ng))
    parts[3] = jnp.concatenate([jnp.pad(parts[3][:, :ng], gpad), jnp.pad(parts[3][:, ng:], gpad)], axis=1)
    weights = [p.astype(BF16) for p in parts]
    g1 = norm1_g[0][None]

    rw_prm = dict(conv=rwkv_conv[0], w0=rwkv_w0[0], w_up=rwkv_w_up[0].astype(BF16), a0=rwkv_a0[0],
                  a_up=rwkv_a_up[0].astype(BF16), k_k=rwkv_k_k[0][None], k_a=rwkv_k_a[0][None], r_k=rwkv_r_k[0][None],
                  ln_w=rwkv_ln_w[0][None], ln_b=rwkv_ln_b[0][None], g_up=rwkv_g_up[0].astype(BF16))
    bias = jnp.concatenate([jnp.pad(mlstm_i_b[0].reshape(1, ng), gpad), jnp.pad(mlstm_f_b[0].reshape(1, ng), gpad)], axis=1)
    ml_prm = dict(conv=mlstm_conv[0], bias=bias, norm_g=mlstm_norm_g[0][None])

    p_rw, p_qk, p_mv, p_if, _, _, _ = _proj(cs, g1, csh1, csc1, weights, _tile(tc, 256))
    s_zero = jnp.zeros((N_DIR, RWKV_PAIRS, LANES, LANES), F32)
    _, _, rw_state = _rwkv(p_rw, rw_prm, s_zero, tc, tc)
    m_zero = (jnp.zeros((ng, MLSTM_DQK, 2 * MLSTM_DV), F32), jnp.zeros((8, LANES), F32))
    _, _, ml_state = _mlstm(p_qk, p_mv, p_if, ml_prm, m_zero, tc, 1)

    p_rw, p_qk, p_mv, p_if, p_o, p_ga, p_gb = _proj(xs, g1, sh1, sc1, weights, _tile(t, 256))
    ya0, ya1, _ = _rwkv(p_rw, rw_prm, rw_state, _tile(t, 256), GRID_W)
    hb0, hb1, _ = _mlstm(p_qk, p_mv, p_if, ml_prm, ml_state, rows, GRID_W)

    x1, h2, q = _merge(xs, ya0, ya1, hb0, hb1, p_o, p_ga, p_gb, w_branch_a[0].astype(BF16), w_branch_b[0].astype(BF16),
                       w_out[0].astype(BF16), gt1, norm2_g[0][None], sh2, sc2, peer_wq[0].astype(BF16), _tile(t, 256))
    r2, e2, n1, e1 = _router(q, peer_keys[0].astype(BF16), _tile(t, 512))
    vt = peer_v[0].astype(BF16).reshape(-1, EXPERT_SUB, D_MODEL).transpose(0, 2, 1)
    out = _experts(h2, peer_u[0].astype(BF16), vt, r2, e2, n1, e1, x1, gt2, final_g[None],
                   _tile(t, 512), 2048)
    return out[None]
```

```python
import functools

import jax
import jax.numpy as jnp
from jax import lax
from jax.experimental import pallas as pl
from jax.experimental.pallas import tpu as pltpu

F32 = jnp.float32
BF16 = jnp.bfloat16

D_MODEL = 1024
GRID_W = 64
N_MOD = 6
NORM_EPS = 1e-6

RWKV_HEAD = 64
RWKV_HEADS = 8
D_RWKV = RWKV_HEADS * RWKV_HEAD
LORA_W = 64
LORA_A = 64
LORA_G = 128
RWKV_COLS = 3 * D_RWKV + LORA_W + LORA_A + LORA_G
GN_EPS = 64e-5
RWKV_PAIRS = RWKV_HEADS // 2

MLSTM_HEADS = 4
MLSTM_DQK = 64
MLSTM_DV = 128
D_MLSTM_QK = MLSTM_HEADS * MLSTM_DQK
D_MLSTM = MLSTM_HEADS * MLSTM_DV
N_DIR = 2
CHUNK = 64

PEER_HEADS = 8
N_KEYS = 128
PEER_TOPK = 16
KEY_DIM = 128
PEER_QDIM = 2 * KEY_DIM

LANES = 128
VMEM_LIMIT = 56 * 1024 * 1024

_HI = lax.Precision.HIGHEST


def _dot(a, b, precision=None):
    return jnp.dot(a, b, preferred_element_type=F32, precision=precision)


def _dot_nt(a, b):
    return lax.dot_general(a, b, (((1,), (1,)), ((), ())), preferred_element_type=F32)


def _dot_tn(a, b):
    return lax.dot_general(a, b, (((0,), (0,)), ((), ())), preferred_element_type=F32)


def _sigmoid(x):
    return 1.0 / (1.0 + jnp.exp(-x))


def _params(sem):
    return pltpu.CompilerParams(dimension_semantics=sem, vmem_limit_bytes=VMEM_LIMIT)


def _scan_rows(x, reverse):
    n = x.shape[0]
    row = lax.broadcasted_iota(jnp.int32, x.shape, 0)
    d = 1
    while d < n:
        if reverse:
            x = x + jnp.where(row < n - d, pltpu.roll(x, n - d, axis=0), 0.0)
        else:
            x = x + jnp.where(row >= d, pltpu.roll(x, d, axis=0), 0.0)
        d *= 2
    return x


def _conv3_rows(x, w, period):
    n = x.shape[0]
    pos = lax.broadcasted_iota(jnp.int32, x.shape, 0) % period
    prev = jnp.where(pos == 0, 0.0, pltpu.roll(x, 1, axis=0))
    nxt = jnp.where(pos == period - 1, 0.0, pltpu.roll(x, n - 1, axis=0))
    return w[0:1] * prev + w[1:2] * x + w[2:3] * nxt


def _ada_kernel(c_ref, w_ref, b_ref, o_ref):
    c = c_ref[...]
    s = c * _sigmoid(c)
    o_ref[...] = _dot(s, w_ref[...], precision=_HI) + b_ref[...]


def _ada(cc, w, b):
    n = w.shape[1]
    bn = n // 4
    return pl.pallas_call(
        _ada_kernel,
        grid=(n // bn,),
        in_specs=[pl.BlockSpec((8, D_MODEL), lambda i: (0, 0)),
                  pl.BlockSpec((D_MODEL, bn), lambda i: (0, i)),
                  pl.BlockSpec((1, bn), lambda i: (0, i))],
        out_specs=pl.BlockSpec((8, bn), lambda i: (0, i)),
        out_shape=jax.ShapeDtypeStruct((8, n), F32),
        compiler_params=_params(("arbitrary",)),
        name="ada",
    )(cc, w, b)


def _proj_kernel(nw, x_ref, g_ref, sh_ref, sc_ref, *refs):
    x = x_ref[...]
    y = x * lax.rsqrt(jnp.mean(x * x, axis=-1, keepdims=True) + NORM_EPS)
    h = (y * g_ref[...]) * (1.0 + sc_ref[...]) + sh_ref[...]
    hb = h.astype(BF16)
    for w_ref, o_ref in zip(refs[:nw], refs[nw:]):
        o_ref[...] = _dot(hb, w_ref[...]).astype(o_ref.dtype)


def _proj(x, g, shift, scale, weights, dtypes, tm):
    t = x.shape[0]
    row = lambda i: (i, 0)
    fix = lambda i: (0, 0)
    in_specs = [pl.BlockSpec((tm, D_MODEL), row)] + [pl.BlockSpec((1, D_MODEL), fix)] * 3
    in_specs += [pl.BlockSpec(w.shape, fix) for w in weights]
    return pl.pallas_call(
        functools.partial(_proj_kernel, len(weights)),
        grid=(t // tm,),
        in_specs=in_specs,
        out_specs=[pl.BlockSpec((tm, w.shape[1]), row) for w in weights],
        out_shape=[jax.ShapeDtypeStruct((t, w.shape[1]), dt) for w, dt in zip(weights, dtypes)],
        compiler_params=_params(("parallel",)),
        name="proj",
    )(x, g, shift, scale, *weights)


def _rwkv_prep(z, f_ref, conv_ref, w0_ref, wup_ref, a0_ref, aup_ref, kk_ref, ka_ref, gup_ref, period, dst):
    f = _conv3_rows(f_ref[...], conv_ref[...], period)
    r = f[:, 0:D_RWKV]
    k = f[:, D_RWKV:2 * D_RWKV]
    v = f[:, 2 * D_RWKV:3 * D_RWKV]
    o = 3 * D_RWKV
    wd = f[:, o:o + LORA_W]
    ad = f[:, o + LORA_W:o + LORA_W + LORA_A]
    gd = f[:, o + LORA_W + LORA_A:o + LORA_W + LORA_A + LORA_G]
    lw = w0_ref[z:z + 1, :] + _dot(jnp.tanh(wd).astype(BF16), wup_ref[z])
    w_log = -(jnp.maximum(-lw, 0.0) + jnp.log(1.0 + jnp.exp(-jnp.abs(lw)))) - 0.5
    a = _sigmoid(a0_ref[z:z + 1, :] + _dot(ad.astype(BF16), aup_ref[z]))
    g = _dot(_sigmoid(gd).astype(BF16), gup_ref[...])
    logw_s, kraw_s, a_s, keff_s, v_s, r_s, g_s = dst
    logw_s[z] = -jnp.exp(w_log)
    kraw_s[z] = k * kk_ref[...]
    a_s[z] = a
    keff_s[z] = k * (1.0 + (a - 1.0) * ka_ref[...])
    v_s[z] = v
    r_s[z] = r
    g_s[z] = g


def _rwkv_setup(units, masks):
    def st(x, lane_lo):
        return jnp.concatenate([jnp.where(lane_lo, x, 0.0), jnp.where(lane_lo, 0.0, x)], axis=0)

    def mm(p, q):
        return _dot(p.astype(BF16), q.astype(BF16))

    def each(f, *lists):
        return [f(*args) for args in zip(*lists)]

    def bcast(col):
        return jnp.broadcast_to(col, (LANES, LANES))

    lo = [masks[u[0]][0] for u in units]
    kr = [st(u[2], m) for u, m in zip(units, lo)]
    ssq = [jnp.sum(x * x, axis=1, keepdims=True) for x in kr]
    bon = [jnp.sum(st(u[6] * u[4] * u[7], m), axis=1, keepdims=True) for u, m in zip(units, lo)]
    inv = [bcast(1.0 / jnp.maximum(jnp.sqrt(x), 1e-12)) for x in ssq]
    bon = [bcast(x) for x in bon]
    pre = []
    for (z, lw, kraw, a, keff, v, r, rk), lane_lo, iv, bo in zip(units, lo, inv, bon):
        rev = z == 1
        cum = _scan_rows(lw, rev)
        tot = cum[0:1, :] if rev else cum[CHUNK - 1:CHUNK, :]
        e_in = jnp.exp(cum)
        e_ex = jnp.exp(cum - lw)
        e_ng = jnp.exp(-cum)
        e_rem = jnp.exp(tot - cum)
        ka = kraw * a
        vs = st(v, lane_lo)
        pre.append(dict(
            z=z, xk=(st(kraw * e_ex, lane_lo) * iv).astype(BF16), xr=st(r * e_in, lane_lo).astype(BF16),
            yk=st(keff * e_ng, lane_lo).astype(BF16), yb=(st(ka * e_ng, lane_lo) * iv).astype(BF16),
            ykg=st(keff * e_rem, lane_lo).astype(BF16), ybg=(st(ka * e_rem, lane_lo) * iv).astype(BF16),
            vs=vs, vb=vs.astype(BF16), gam=jnp.exp(tot), bonus=bo * vs))
    zs = [p["z"] for p in pre]

    m = [_dot_nt(jnp.concatenate([p["xk"], p["xr"]], axis=0), jnp.concatenate([p["yk"], p["yb"]], axis=0)) for p in pre]
    akk = [jnp.where(masks[z][2], x[:LANES, :LANES], 0.0).astype(BF16) for z, x in zip(zs, m)]
    ark = [jnp.where(masks[z][3], x[LANES:, :LANES], 0.0).astype(BF16) for z, x in zip(zs, m)]
    arb = [jnp.where(masks[z][3], x[LANES:, LANES:], 0.0).astype(BF16) for z, x in zip(zs, m)]
    a_d = [jnp.where(masks[z][2] & masks[z][4], x[:LANES, LANES:], 0.0) for z, x in zip(zs, m)]
    a_off = [jnp.where(masks[z][2] & jnp.logical_not(masks[z][4]), x[:LANES, LANES:], 0.0) for z, x in zip(zs, m)]
    eye = masks[0][5]

    a2 = each(mm, a_d, a_d)
    t_d = [eye - x for x in a_d]
    avk = [_dot(k, p["vb"]) for k, p in zip(akk, pre)]
    a4 = each(mm, a2, a2)
    t_d = each(lambda t, x: t + mm(t, x), t_d, a2)
    avr = [_dot(k, p["vb"]) for k, p in zip(ark, pre)]
    a8 = each(mm, a4, a4)
    t_d = each(lambda t, x: t + mm(t, x), t_d, a4)
    ds0 = [_dot_tn(p["vb"], p["ykg"]) for p in pre]
    t_d = each(lambda t, x: t + mm(t, x), t_d, a8)
    n1 = each(mm, t_d, a_off)
    n2 = each(mm, n1, n1)
    n3 = each(mm, n1, n2)
    tmat = each(lambda x1, x2, x3, t: mm(eye - x1 + x2 - x3, t).astype(BF16), n1, n2, n3, t_d)
    return [dict(z=p["z"], xk=p["xk"], xr=p["xr"], ybg=p["ybg"], vs=p["vs"], gam=p["gam"], bonus=p["bonus"],
                 tmat=t, avk=k, avr=r, arb=b, ds0=d)
            for p, t, k, r, b, d in zip(pre, tmat, avk, avr, arb, ds0)]


def _rwkv_advance(pre, s_prev, lnw, lnb, masks):
    sb = [s.astype(BF16) for s in s_prev]
    rhs = [_dot_nt(p["xk"], s) + p["avk"] for p, s in zip(pre, sb)]
    osr = [_dot_nt(p["xr"], s) + p["avr"] for p, s in zip(pre, sb)]
    ub = [_dot(p["tmat"], x.astype(BF16)).astype(BF16) for p, x in zip(pre, rhs)]
    o = [x - _dot(p["arb"], u) for x, p, u in zip(osr, pre, ub)]
    s_new = [s * p["gam"] + p["ds0"] - _dot_tn(u, p["ybg"]) for s, p, u in zip(s_prev, pre, ub)]
    full = (LANES, LANES)
    own = [masks[p["z"]][1] for p in pre]
    mu = [jnp.broadcast_to(jnp.sum(x, axis=1, keepdims=True) * (1.0 / RWKV_HEAD), full) for x in o]
    cen = [jnp.where(m, x - y, 0.0) for m, x, y in zip(own, o, mu)]
    var = [jnp.sum(x * x, axis=1, keepdims=True) * (1.0 / RWKV_HEAD) for x in cen]
    rstd = [jnp.broadcast_to(lax.rsqrt(x + GN_EPS), full) for x in var]
    y = [c * r * w + jnp.where(m, b, 0.0) + p["bonus"] for c, r, w, m, b, p in zip(cen, rstd, lnw, own, lnb, pre)]
    return [x[:CHUNK] + x[CHUNK:] for x in y], s_new


def _rwkv_masks():
    i = lax.broadcasted_iota(jnp.int32, (LANES, LANES), 0)
    j = lax.broadcasted_iota(jnp.int32, (LANES, LANES), 1)
    lane_lo = lax.broadcasted_iota(jnp.int32, (CHUNK, LANES), 1) < RWKV_HEAD
    same = (i // CHUNK) == (j // CHUNK)
    eye = jnp.where(i == j, 1.0, 0.0).astype(F32)
    diag16 = (i // 16) == (j // 16)
    out = []
    for z in range(N_DIR):
        before = (j % CHUNK) > (i % CHUNK) if z == 1 else (j % CHUNK) < (i % CHUNK)
        strict = same & before
        incl = same & (before | (i == j))
        out.append((lane_lo, same, strict, incl, diag16, eye))
    return out


def _rwkv_kernel(period, nch, f0_ref, f1_ref, conv_ref, w0_ref, wup_ref, a0_ref, aup_ref, kk_ref, ka_ref, rk_ref,
                 lnw_ref, lnb_ref, gup_ref, sin_ref, y0_ref, y1_ref, sout_ref,
                 s_ref, logw_s, kraw_s, a_s, keff_s, v_s, r_s, g_s):
    step = pl.program_id(0)

    @pl.when(step == 0)
    def _():
        s_ref[...] = sin_ref[...]

    dst = (logw_s, kraw_s, a_s, keff_s, v_s, r_s, g_s)
    for z, f_ref in ((0, f0_ref), (1, f1_ref)):
        _rwkv_prep(z, f_ref, conv_ref, w0_ref, wup_ref, a0_ref, aup_ref, kk_ref, ka_ref, gup_ref, period, dst)

    masks = _rwkv_masks()
    y_refs = (y0_ref, y1_ref)

    units, where = [], []
    for ci in range(nch):
        for z in range(N_DIR):
            r0 = (nch - 1 - ci if z == 1 else ci) * CHUNK
            rows = slice(r0, r0 + CHUNK)
            for p in range(RWKV_PAIRS):
                ls = slice(p * LANES, (p + 1) * LANES)
                units.append((z, logw_s[z, rows, ls], kraw_s[z, rows, ls], a_s[z, rows, ls], keff_s[z, rows, ls],
                              v_s[z, rows, ls], r_s[z, rows, ls], rk_ref[:, ls]))
                where.append((z, p, rows, ls))
    pre = _rwkv_setup(units, masks)
    per = N_DIR * RWKV_PAIRS
    state = [s_ref[z, p] for z in range(N_DIR) for p in range(RWKV_PAIRS)]
    lnw = [lnw_ref[:, p * LANES:(p + 1) * LANES] for z in range(N_DIR) for p in range(RWKV_PAIRS)]
    lnb = [lnb_ref[:, p * LANES:(p + 1) * LANES] for z in range(N_DIR) for p in range(RWKV_PAIRS)]
    for ci in range(nch):
        ys, state = _rwkv_advance(pre[ci * per:(ci + 1) * per], state, lnw, lnb, masks)
        for (z, p, rows, ls), y in zip(where[ci * per:(ci + 1) * per], ys):
            y_refs[z][rows, ls] = y * g_s[z, rows, ls]
    for i, s_new in enumerate(state):
        s_ref[i // RWKV_PAIRS, i % RWKV_PAIRS] = s_new

    @pl.when(step == pl.num_programs(0) - 1)
    def _():
        sout_ref[...] = s_ref[...]


def _rwkv(feat, prm, s_init, tb, period):
    t = feat.shape[0]
    nb = t // tb
    nch = tb // CHUNK
    fix2 = lambda i: (0, 0)
    fix3 = lambda i: (0, 0, 0)
    fix4 = lambda i: (0, 0, 0, 0)
    fwd = lambda i: (i, 0)
    bwd = lambda i: (nb - 1 - i, 0)
    vec = pl.BlockSpec((1, D_RWKV), fix2)
    state = pl.BlockSpec((N_DIR, RWKV_PAIRS, LANES, LANES), fix4)
    big = pltpu.VMEM((N_DIR, tb, D_RWKV), F32)
    return pl.pallas_call(
        functools.partial(_rwkv_kernel, period, nch),
        grid=(nb,),
        in_specs=[pl.BlockSpec((tb, RWKV_COLS), fwd), pl.BlockSpec((tb, RWKV_COLS), bwd),
                  pl.BlockSpec((3, RWKV_COLS), fix2),
                  pl.BlockSpec((N_DIR, D_RWKV), fix2), pl.BlockSpec((N_DIR, LORA_W, D_RWKV), fix3),
                  pl.BlockSpec((N_DIR, D_RWKV), fix2), pl.BlockSpec((N_DIR, LORA_A, D_RWKV), fix3),
                  vec, vec, vec, vec, vec,
                  pl.BlockSpec((LORA_G, D_RWKV), fix2), state],
        out_specs=[pl.BlockSpec((tb, D_RWKV), fwd), pl.BlockSpec((tb, D_RWKV), bwd), state],
        out_shape=[jax.ShapeDtypeStruct((t, D_RWKV), F32), jax.ShapeDtypeStruct((t, D_RWKV), F32),
                   jax.ShapeDtypeStruct((N_DIR, RWKV_PAIRS, LANES, LANES), F32)],
        scratch_shapes=[pltpu.VMEM((N_DIR, RWKV_PAIRS, LANES, LANES), F32)] + [big] * 7,
        compiler_params=_params(("arbitrary",)),
        name="rwkv",
    )(feat, feat, prm["conv"], prm["w0"], prm["w_up"], prm["a0"], prm["a_up"], prm["k_k"], prm["k_a"], prm["r_k"],
      prm["ln_w"], prm["ln_b"], prm["g_up"], s_init)


def _cummax_rows(x, reverse):
    n = x.shape[0]
    row = lax.broadcasted_iota(jnp.int32, x.shape, 0)
    d = 1
    while d < n:
        if reverse:
            x = jnp.maximum(x, jnp.where(row < n - d, pltpu.roll(x, n - d, axis=0), -jnp.inf))
        else:
            x = jnp.maximum(x, jnp.where(row >= d, pltpu.roll(x, d, axis=0), -jnp.inf))
        d *= 2
    return x


def _mlstm_kernel(nch, qk0_ref, qk1_ref, mv0_ref, mv1_ref, if0_ref, if1_ref, conv_ref, bias_ref, ng_ref,
                  cin_ref, min_ref, h0_ref, h1_ref, cout_ref, mout_ref,
                  c_s, m_s, qk_s, gi_s, lf_s):
    step = pl.program_id(0)
    rows_n = qk0_ref.shape[0]
    nh = MLSTM_HEADS

    @pl.when(step == 0)
    def _():
        c_s[...] = cin_ref[...]
        m_s[...] = min_ref[...]

    for z, (qk_ref, if_ref) in enumerate(((qk0_ref, if0_ref), (qk1_ref, if1_ref))):
        qk = _conv3_rows(qk_ref[...], conv_ref[...], rows_n)
        qk_s[z] = qk * _sigmoid(qk)
        gate = if_ref[...] + bias_ref[...]
        gi_s[z] = gate[:, :LANES]
        fg = gate[:, LANES:]
        lf_s[z] = jnp.minimum(fg, 0.0) - jnp.log(1.0 + jnp.exp(-jnp.abs(fg)))

    ti = lax.broadcasted_iota(jnp.int32, (CHUNK, CHUNK), 0)
    si = lax.broadcasted_iota(jnp.int32, (CHUNK, CHUNK), 1)
    causal = (si <= ti, si >= ti)
    lane = lax.broadcasted_iota(jnp.int32, (1, LANES), 1)
    mv_refs = (mv0_ref, mv1_ref)
    h_refs = (h0_ref, h1_ref)
    ones = jnp.ones((CHUNK, LANES), BF16)
    lane0 = lax.broadcasted_iota(jnp.int32, (CHUNK, LANES), 1) == 0

    mrow = m_s[0:1, :]
    us = []
    for ci in range(nch):
        m_next = mrow
        for z in range(N_DIR):
            rev = z == 1
            r0 = (nch - 1 - ci if rev else ci) * CHUNK
            rows = slice(r0, r0 + CHUNK)
            gi = gi_s[z, rows, :]
            b = _scan_rows(lf_s[z, rows, :], rev)
            d = gi - b
            bend = b[0:1, :] if rev else b[CHUNK - 1:CHUNK, :]
            m_row = b + jnp.maximum(mrow, _cummax_rows(d, rev))
            a_int = jnp.exp(b + mrow - m_row)
            c1 = b - m_row
            g_end = bend - b + gi
            m_new = jnp.maximum(bend + mrow, jnp.max(g_end, axis=0, keepdims=True))
            w_end = jnp.exp(g_end - m_new)
            keep = jnp.exp(bend + mrow - m_new)
            unit_lanes = (lane >= z * nh) & (lane < (z + 1) * nh)
            m_next = jnp.where(unit_lanes, m_new, m_next)
            d_t = d.T
            for h in range(nh):
                u = z * nh + h
                us.append(dict(
                    ci=ci, z=z, h=h, u=u, rows=rows, c1=c1[:, u:u + 1], drow=d_t[u:u + 1, :], a_int=a_int[:, u:u + 1],
                    w_end=w_end[:, u:u + 1], keep=keep[:, u:u + 1], einv=jnp.exp(-m_row[:, u:u + 1]),
                    q=qk_s[z, rows, h * MLSTM_DQK:(h + 1) * MLSTM_DQK],
                    k=qk_s[z, rows, D_MLSTM_QK + h * MLSTM_DQK:D_MLSTM_QK + (h + 1) * MLSTM_DQK] * (MLSTM_DQK ** -0.5),
                    v=mv_refs[z][rows, h * MLSTM_DV:(h + 1) * MLSTM_DV]))
        mrow = m_next
    def col(key, width):
        return [jnp.broadcast_to(x[key], (CHUNK, width)) for x in us]

    qb = [x["q"].astype(BF16) for x in us]
    kb = [x["k"].astype(BF16) for x in us]
    qk = [_dot_nt(x, y) for x, y in zip(qb, kb)]
    web = col("w_end", MLSTM_DV)
    dc = [_dot_tn(x, jnp.concatenate([(y["v"] * w).astype(BF16), jnp.where(lane0, w, 0.0).astype(BF16)], axis=1))
          for x, y, w in zip(kb, us, web)]
    c1b = col("c1", CHUNK)
    smat = [x * jnp.exp(jnp.where(causal[y["z"]], c + y["drow"], -jnp.inf)) for x, y, c in zip(qk, us, c1b)]
    sv = [_dot(x.astype(BF16), jnp.concatenate([y["v"].astype(BF16), ones], axis=1)) for x, y in zip(smat, us)]
    keepb = col("keep", 2 * MLSTM_DV)
    ct = [c_s[u] for u in range(N_DIR * nh)]
    qc = []
    for i, x in enumerate(us):
        qc.append(_dot(qb[i], ct[x["u"]].astype(BF16)))
        ct[x["u"]] = keepb[i] * ct[x["u"]] + dc[i]
    aib = col("a_int", 2 * MLSTM_DV)
    tot = [y + a * w for y, a, w in zip(sv, aib, qc)]
    rden = [1.0 / jnp.maximum(jnp.abs(x[:, MLSTM_DV:MLSTM_DV + 1]), y["einv"]) for x, y in zip(tot, us)]
    rden = [jnp.broadcast_to(x, (CHUNK, MLSTM_DV)) for x in rden]
    hh = [x[:, :MLSTM_DV] * r for x, r in zip(tot, rden)]
    hms = [jnp.mean(x * x, axis=1, keepdims=True) for x in hh]
    rstd = [jnp.broadcast_to(lax.rsqrt(x + NORM_EPS), (CHUNK, MLSTM_DV)) for x in hms]
    for x, y, r in zip(us, hh, rstd):
        cols = slice(x["h"] * MLSTM_DV, (x["h"] + 1) * MLSTM_DV)
        h_refs[x["z"]][x["rows"], cols] = (y * r * ng_ref[:, cols]).astype(BF16)
    for u, c in enumerate(ct):
        c_s[u] = c
    m_s[...] = jnp.broadcast_to(mrow, m_s.shape)

    @pl.when(step == pl.num_programs(0) - 1)
    def _():
        cout_ref[...] = c_s[...]
        mout_ref[...] = m_s[...]


def _mlstm(qk, mv, gates, prm, state, rows, ncol):
    nch = rows // CHUNK
    nu = N_DIR * MLSTM_HEADS
    qk2 = qk.reshape(rows, ncol * 2 * D_MLSTM_QK)
    mv2 = mv.reshape(rows, ncol * D_MLSTM)
    g2 = gates.reshape(rows, ncol * 2 * LANES)
    fwd = lambda i: (0, i)
    bwd = lambda i: (0, ncol - 1 - i)
    fix2 = lambda i: (0, 0)
    fix3 = lambda i: (0, 0, 0)
    cspec = pl.BlockSpec((nu, MLSTM_DQK, 2 * MLSTM_DV), fix3)
    mspec = pl.BlockSpec((8, LANES), fix2)
    wide = lambda m: pl.BlockSpec((rows, 2 * D_MLSTM_QK), m)
    outs = pl.pallas_call(
        functools.partial(_mlstm_kernel, nch),
        grid=(ncol,),
        in_specs=[wide(fwd), wide(bwd), wide(fwd), wide(bwd),
                  pl.BlockSpec((rows, 2 * LANES), fwd), pl.BlockSpec((rows, 2 * LANES), bwd),
                  pl.BlockSpec((3, 2 * D_MLSTM_QK), fix2), pl.BlockSpec((1, 2 * LANES), fix2),
                  pl.BlockSpec((1, D_MLSTM), fix2), cspec, mspec],
        out_specs=[wide(fwd), wide(bwd), cspec, mspec],
        out_shape=[jax.ShapeDtypeStruct((rows, ncol * D_MLSTM), BF16)] * 2
        + [jax.ShapeDtypeStruct((nu, MLSTM_DQK, 2 * MLSTM_DV), F32), jax.ShapeDtypeStruct((8, LANES), F32)],
        scratch_shapes=[pltpu.VMEM((nu, MLSTM_DQK, 2 * MLSTM_DV), F32), pltpu.VMEM((8, LANES), F32),
                        pltpu.VMEM((N_DIR, rows, 2 * D_MLSTM_QK), F32),
                        pltpu.VMEM((N_DIR, rows, LANES), F32), pltpu.VMEM((N_DIR, rows, LANES), F32)],
        compiler_params=_params(("arbitrary",)),
        name="mlstm",
    )(qk2, qk2, mv2, mv2, g2, g2, prm["conv"], prm["bias"], prm["norm_g"], *state)
    h0, h1 = outs[0].reshape(rows * ncol, D_MLSTM), outs[1].reshape(rows * ncol, D_MLSTM)
    return h0, h1, tuple(outs[2:])


def _merge_kernel(x_ref, ya0_ref, ya1_ref, hb0_ref, hb1_ref, po_ref, ga_ref, gb_ref, wa_ref, wb_ref, wo_ref,
                  gt1_ref, g2_ref, sh2_ref, sc2_ref, wq_ref, x1_ref, h2_ref, q_ref):
    ya = ya0_ref[...] + ya1_ref[...]
    yb = (hb0_ref[...].astype(F32) + hb1_ref[...].astype(F32)) * _sigmoid(po_ref[...])
    merged = (_sigmoid(ga_ref[...]) * _dot(ya.astype(BF16), wa_ref[...])
              + _sigmoid(gb_ref[...]) * _dot(yb.astype(BF16), wb_ref[...]))
    x1 = x_ref[...] + gt1_ref[...] * _dot(merged.astype(BF16), wo_ref[...])
    x1_ref[...] = x1
    y = x1 * lax.rsqrt(jnp.mean(x1 * x1, axis=-1, keepdims=True) + NORM_EPS)
    h2 = ((y * g2_ref[...]) * (1.0 + sc2_ref[...]) + sh2_ref[...]).astype(BF16)
    h2_ref[...] = h2
    q_ref[...] = _dot(h2, wq_ref[...]).astype(q_ref.dtype)


def _merge(x, ya0, ya1, hb0, hb1, po, ga, gb, wa, wb, wo, gt1, g2, sh2, sc2, wq, tm):
    t = x.shape[0]
    row = lambda i: (i, 0)
    fix = lambda i: (0, 0)
    rs = lambda n: pl.BlockSpec((tm, n), row)
    vec = pl.BlockSpec((1, D_MODEL), fix)
    nq = wq.shape[1]
    return pl.pallas_call(
        _merge_kernel,
        grid=(t // tm,),
        in_specs=[rs(D_MODEL), rs(D_RWKV), rs(D_RWKV), rs(D_MLSTM), rs(D_MLSTM), rs(D_MLSTM), rs(D_MODEL), rs(D_MODEL),
                  pl.BlockSpec(wa.shape, fix), pl.BlockSpec(wb.shape, fix), pl.BlockSpec(wo.shape, fix),
                  vec, vec, vec, vec, pl.BlockSpec(wq.shape, fix)],
        out_specs=[rs(D_MODEL), rs(D_MODEL), rs(nq)],
        out_shape=[jax.ShapeDtypeStruct((t, D_MODEL), F32), jax.ShapeDtypeStruct((t, D_MODEL), BF16),
                   jax.ShapeDtypeStruct((t, nq), BF16)],
        compiler_params=_params(("parallel",)),
        name="merge",
    )(x, ya0, ya1, hb0, hb1, po, ga, gb, wa, wb, wo, gt1, g2, sh2, sc2, wq)


def _sort_pairs(n):
    pairs = []
    t = max(1, (n - 1).bit_length())
    p = 1 << (t - 1)
    while p > 0:
        q, r, d = 1 << (t - 1), 0, p
        while d > 0:
            pairs += [(i, i + d) for i in range(n - d) if (i & p) == r]
            d, q, r = q - p, q >> 1, p
        p >>= 1
    return pairs


def _bitonic_desc(c):
    n = len(c)
    d = n // 2
    while d > 0:
        for i in range(n):
            if (i & d) == 0:
                c[i], c[i + d] = jnp.maximum(c[i], c[i + d]), jnp.minimum(c[i], c[i + d])
        d //= 2
    return c


def _merge_top(x, y):
    n = len(x)
    return _bitonic_desc([jnp.maximum(x[i], y[n - 1 - i]) for i in range(n)])


def _top16_levels(scores):
    x = [scores[8 * i:8 * (i + 1), :] for i in range(N_KEYS // 8)]
    for i, j in _sort_pairs(len(x)):
        x[i], x[j] = jnp.maximum(x[i], x[j]), jnp.minimum(x[i], x[j])
    for shift in (4, 2, 1):
        x = _merge_top(x, [pltpu.roll(v, shift, axis=0) for v in x])
    return x


def _router_kernel(q_ref, keys_ref, r2_ref, e2_ref, n1_ref, e1_ref):
    qb = q_ref[...]
    s1 = _dot_nt(keys_ref[0, 0], qb[:, :KEY_DIM])
    s2 = _dot_nt(keys_ref[0, 1], qb[:, KEY_DIM:])
    a = _top16_levels(s1)
    b = _top16_levels(s2)
    k = PEER_TOPK
    nj = [k // (i + 1) for i in range(k)]
    cell = {(i, j): a[i] + b[j] for i in range(k) for j in range(nj[i])}
    ninf = jnp.full(a[0].shape, -jnp.inf, F32)
    pad = lambda lst: lst + [ninf] * (k - len(lst))
    lists = [pad([cell[(i, j)] for j in range(nj[i])]) for i in range(4)]
    lists += [pad([cell[(i, j)] for i in range(4, k) if j < nj[i]]) for j in range(3)]
    top = lists[0]
    for other in lists[1:]:
        top = _merge_top(top, other)
    tau = top[k - 1]
    mx = cell[(0, 0)]
    zsum = jnp.zeros_like(tau)
    n_rank = []
    for i in range(k):
        cnt = jnp.zeros_like(tau)
        for j in range(nj[i]):
            sel = cell[(i, j)] >= tau
            zsum = zsum + jnp.where(sel, jnp.exp(cell[(i, j)] - mx), 0.0)
            cnt = cnt + jnp.where(sel, 1.0, 0.0)
        n_rank.append(cnt)
    rz = 1.0 / zsum
    for blk in range(N_KEYS // 16):
        r2, e2 = [], []
        for half in range(2):
            rows = slice(16 * blk + 8 * half, 16 * blk + 8 * (half + 1))
            s1b, s2b = s1[rows, :], s2[rows, :]
            n1 = jnp.zeros_like(s1b)
            for i in reversed(range(k)):
                n1 = jnp.where(s1b == a[i], n_rank[i], n1)
            n1_ref[0, rows, :] = n1
            e1_ref[0, rows, :] = jnp.exp(s1b - a[0]) * rz
            rank = jnp.zeros_like(s2b)
            for lvl in b:
                rank = rank + jnp.where(lvl > s2b, 1.0, 0.0)
            r2.append(rank)
            e2.append(jnp.exp(s2b - b[0]))
        rows = slice(16 * blk, 16 * (blk + 1))
        r2_ref[0, rows, :] = jnp.concatenate(r2, axis=0).astype(BF16)
        e2_ref[0, rows, :] = jnp.concatenate(e2, axis=0).astype(BF16)


def _router(q, keys, tt):
    t = q.shape[0]
    spec = pl.BlockSpec((1, N_KEYS, tt), lambda i, h: (h, 0, i))
    shape = lambda dt: jax.ShapeDtypeStruct((PEER_HEADS, N_KEYS, t), dt)
    return pl.pallas_call(
        _router_kernel,
        grid=(t // tt, PEER_HEADS),
        in_specs=[pl.BlockSpec((tt, PEER_QDIM), lambda i, h: (i, h)),
                  pl.BlockSpec((1, 2, N_KEYS, KEY_DIM), lambda i, h: (h, 0, 0, 0))],
        out_specs=[spec] * 4,
        out_shape=[shape(BF16), shape(BF16), shape(F32), shape(F32)],
        compiler_params=_params(("parallel", "parallel")),
        name="router",
    )(q, keys)


EXPERT_SUB = 512
TILE16 = 16
GATE_TOKENS = 512


def _experts_kernel(nsub, h2_ref, u_ref, vt_ref, r2_ref, e2_ref, n1_ref, e1_ref, x1_ref, gt2_ref, fg_ref, o_ref,
                    acc_ref, act_ref, w_ref):
    e = pl.program_id(1)
    na = EXPERT_SUB // N_KEYS
    tt = h2_ref.shape[0]

    @pl.when(e == 0)
    def _():
        acc_ref[...] = jnp.zeros_like(acc_ref)

    def first_dot(si, dst):
        act_ref[dst] = _dot_nt(u_ref[si * EXPERT_SUB:(si + 1) * EXPERT_SUB, :], h2_ref[...])

    def build(si, cur):
        for ai in range(na):
            a = si * na + ai
            rows_a = slice(ai * N_KEYS, (ai + 1) * N_KEYS)
            for tb in range(tt // GATE_TOKENS):
                tok = slice(tb * GATE_TOKENS, (tb + 1) * GATE_TOKENS)
                gate = [jnp.zeros((TILE16, GATE_TOKENS), BF16)] * (N_KEYS // TILE16)
                for h in range(PEER_HEADS):
                    n1 = jnp.broadcast_to(n1_ref[h, a:a + 1, tok], (TILE16, GATE_TOKENS)).astype(BF16)
                    e1 = jnp.broadcast_to(e1_ref[h, a:a + 1, tok], (TILE16, GATE_TOKENS)).astype(BF16)
                    for rb in range(N_KEYS // TILE16):
                        rows = slice(rb * TILE16, (rb + 1) * TILE16)
                        zero = jnp.zeros((TILE16, GATE_TOKENS), BF16)
                        gate[rb] = gate[rb] + jnp.where(r2_ref[h, rows, tok] < n1, e2_ref[h, rows, tok], zero) * e1
                act = act_ref[cur, rows_a, tok]
                gl = (0.5 * act * (1.0 + lax.erf(act * (2.0 ** -0.5)))).astype(BF16)
                w_ref[cur, rows_a, tok] = jnp.concatenate(gate, axis=0) * gl

    first_dot(0, 0)
    total = None
    for si in range(nsub):
        cur = si % 2
        if si + 1 < nsub:
            first_dot(si + 1, 1 - cur)
        build(si, cur)
        part = _dot(vt_ref[si], w_ref[cur])
        total = part if total is None else total + part
    acc_ref[...] += total

    @pl.when(e == pl.num_programs(1) - 1)
    def _():
        x2 = x1_ref[...] + gt2_ref[...] * acc_ref[...].T
        y = x2 * lax.rsqrt(jnp.mean(x2 * x2, axis=-1, keepdims=True) + NORM_EPS)
        o_ref[...] = y * fg_ref[...]


def _experts(h2, u, vt, r2, e2, n1, e1, x1, gt2, fg, tt, et):
    t = h2.shape[0]
    ne = u.shape[0]
    tok = lambda i, e: (i, 0)
    fix = lambda i, e: (0, 0)
    rt = pl.BlockSpec((PEER_HEADS, N_KEYS, tt), lambda i, e: (0, 0, i))
    rs = pl.BlockSpec((PEER_HEADS, et // N_KEYS, tt), lambda i, e: (0, e, i))
    return pl.pallas_call(
        functools.partial(_experts_kernel, et // EXPERT_SUB),
        grid=(t // tt, ne // et),
        in_specs=[pl.BlockSpec((tt, D_MODEL), tok),
                  pl.BlockSpec((et, D_MODEL), lambda i, e: (e, 0)),
                  pl.BlockSpec((et // EXPERT_SUB, D_MODEL, EXPERT_SUB), lambda i, e: (e, 0, 0)),
                  rt, rt, rs, rs,
                  pl.BlockSpec((tt, D_MODEL), tok),
                  pl.BlockSpec((1, D_MODEL), fix), pl.BlockSpec((1, D_MODEL), fix)],
        out_specs=pl.BlockSpec((tt, D_MODEL), tok),
        out_shape=jax.ShapeDtypeStruct((t, D_MODEL), F32),
        scratch_shapes=[pltpu.VMEM((D_MODEL, tt), F32), pltpu.VMEM((2, EXPERT_SUB, tt), F32),
                        pltpu.VMEM((2, EXPERT_SUB, tt), BF16)],
        compiler_params=_params(("parallel", "arbitrary")),
        name="experts",
    )(h2, u, vt, r2, e2, n1, e1, x1, gt2, fg)


def _tile(n, pref):
    return pref if n % pref == 0 else n


def kernel(x, c, ctx, c_ctx, ada_w, ada_b, norm1_g, w_in, rwkv_conv, rwkv_w0, rwkv_w_up, rwkv_a0, rwkv_a_up, rwkv_g_up, rwkv_k_k, rwkv_k_a, rwkv_r_k, rwkv_ln_w, rwkv_ln_b, mlstm_conv, mlstm_i_b, mlstm_f_b, mlstm_norm_g, w_branch_a, w_branch_b, w_out, norm2_g, peer_wq, peer_keys, peer_u, peer_v, final_g):
    assert x.shape[0] == 1 and ada_w.shape[0] == 1, "one layer, batch 1"
    t, tc = x.shape[1], ctx.shape[1]
    rows = t // GRID_W
    xs, cs = x[0], ctx[0]

    cc = jnp.zeros((8, D_MODEL), F32).at[0].set(c[0]).at[1].set(c_ctx)
    mods = _ada(cc, ada_w[0], ada_b[0][None])
    sh1, sc1, gt1, sh2, sc2, gt2 = [mods[0:1, i * D_MODEL:(i + 1) * D_MODEL] for i in range(N_MOD)]
    csh1, csc1 = mods[1:2, 0:D_MODEL], mods[1:2, D_MODEL:2 * D_MODEL]

    w = w_in[0]
    o = 0
    parts = []
    for n in (RWKV_COLS, 2 * D_MLSTM_QK, D_MLSTM, 2 * N_DIR * MLSTM_HEADS, D_MLSTM, D_MODEL, D_MODEL):
        parts.append(w[:, o:o + n])
        o += n
    ng = N_DIR * MLSTM_HEADS
    gpad = ((0, 0), (0, LANES - ng))
    parts[3] = jnp.concatenate([jnp.pad(parts[3][:, :ng], gpad), jnp.pad(parts[3][:, ng:], gpad)], axis=1)
    weights = [p.astype(BF16) for p in parts]
    out_dt = [F32, F32, BF16, F32, F32, F32, F32]
    g1 = norm1_g[0][None]

    rw_prm = dict(conv=rwkv_conv[0], w0=rwkv_w0[0], w_up=rwkv_w_up[0].astype(BF16), a0=rwkv_a0[0],
                  a_up=rwkv_a_up[0].astype(BF16), k_k=rwkv_k_k[0][None], k_a=rwkv_k_a[0][None], r_k=rwkv_r_k[0][None],
                  ln_w=rwkv_ln_w[0][None], ln_b=rwkv_ln_b[0][None], g_up=rwkv_g_up[0].astype(BF16))
    bias = jnp.concatenate([jnp.pad(mlstm_i_b[0].reshape(1, ng), gpad), jnp.pad(mlstm_f_b[0].reshape(1, ng), gpad)], axis=1)
    ml_prm = dict(conv=mlstm_conv[0], bias=bias, norm_g=mlstm_norm_g[0][None])

    p_rw, p_qk, p_mv, p_if, _, _, _ = _proj(cs, g1, csh1, csc1, weights, out_dt, _tile(tc, 256))
    s_zero = jnp.zeros((N_DIR, RWKV_PAIRS, LANES, LANES), F32)
    _, _, rw_state = _rwkv(p_rw, rw_prm, s_zero, tc, tc)
    m_zero = (jnp.zeros((ng, MLSTM_DQK, 2 * MLSTM_DV), F32), jnp.zeros((8, LANES), F32))
    _, _, ml_state = _mlstm(p_qk, p_mv, p_if, ml_prm, m_zero, tc, 1)

    p_rw, p_qk, p_mv, p_if, p_o, p_ga, p_gb = _proj(xs, g1, sh1, sc1, weights, out_dt, _tile(t, 256))
    ya0, ya1, _ = _rwkv(p_rw, rw_prm, rw_state, _tile(t, 256), GRID_W)
    hb0, hb1, _ = _mlstm(p_qk, p_mv, p_if, ml_prm, ml_state, rows, GRID_W)

    x1, h2, q = _merge(xs, ya0, ya1, hb0, hb1, p_o, p_ga, p_gb, w_branch_a[0].astype(BF16), w_branch_b[0].astype(BF16),
                       w_out[0].astype(BF16), gt1, norm2_g[0][None], sh2, sc2, peer_wq[0].astype(BF16), _tile(t, 256))
    r2, e2, n1, e1 = _router(q, peer_keys[0].astype(BF16), _tile(t, 512))
    vt = peer_v[0].astype(BF16).reshape(-1, EXPERT_SUB, D_MODEL).transpose(0, 2, 1)
    out = _experts(h2, peer_u[0].astype(BF16), vt, r2, e2, n1, e1, x1, gt2, final_g[None],
                   _tile(t, 512), 2048)
    return out[None]
```

```python
import functools

import jax
import jax.numpy as jnp
from jax import lax
from jax.experimental import pallas as pl
from jax.experimental.pallas import tpu as pltpu

F32 = jnp.float32
BF16 = jnp.bfloat16

D_MODEL = 1024
GRID_W = 64
N_MOD = 6
NORM_EPS = 1e-6

RWKV_HEAD = 64
RWKV_HEADS = 8
D_RWKV = RWKV_HEADS * RWKV_HEAD
LORA_W = 64
LORA_A = 64
LORA_G = 128
RWKV_COLS = 3 * D_RWKV + LORA_W + LORA_A + LORA_G
GN_EPS = 64e-5
RWKV_PAIRS = RWKV_HEADS // 2

MLSTM_HEADS = 4
MLSTM_DQK = 64
MLSTM_DV = 128
D_MLSTM_QK = MLSTM_HEADS * MLSTM_DQK
D_MLSTM = MLSTM_HEADS * MLSTM_DV
N_DIR = 2
CHUNK = 64

PEER_HEADS = 8
N_KEYS = 128
PEER_TOPK = 16
KEY_DIM = 128
PEER_QDIM = 2 * KEY_DIM

LANES = 128
VMEM_LIMIT = 56 * 1024 * 1024

_HI = lax.Precision.HIGHEST


def _dot(a, b, precision=None):
    return jnp.dot(a, b, preferred_element_type=F32, precision=precision)


def _dot_nt(a, b):
    return lax.dot_general(a, b, (((1,), (1,)), ((), ())), preferred_element_type=F32)


def _dot_tn(a, b):
    return lax.dot_general(a, b, (((0,), (0,)), ((), ())), preferred_element_type=F32)


def _sigmoid(x):
    return 1.0 / (1.0 + jnp.exp(-x))


def _params(sem):
    return pltpu.CompilerParams(dimension_semantics=sem, vmem_limit_bytes=VMEM_LIMIT)


def _scan_rows(x, reverse):
    n = x.shape[0]
    row = lax.broadcasted_iota(jnp.int32, x.shape, 0)
    d = 1
    while d < n:
        if reverse:
            x = x + jnp.where(row < n - d, pltpu.roll(x, n - d, axis=0), 0.0)
        else:
            x = x + jnp.where(row >= d, pltpu.roll(x, d, axis=0), 0.0)
        d *= 2
    return x


def _conv3_rows(x, w, period):
    n = x.shape[0]
    pos = lax.broadcasted_iota(jnp.int32, x.shape, 0) % period
    prev = jnp.where(pos == 0, 0.0, pltpu.roll(x, 1, axis=0))
    nxt = jnp.where(pos == period - 1, 0.0, pltpu.roll(x, n - 1, axis=0))
    return w[0:1] * prev + w[1:2] * x + w[2:3] * nxt


def _ada_kernel(c_ref, w_ref, b_ref, o_ref):
    c = c_ref[...]
    s = c * _sigmoid(c)
    o_ref[...] = _dot(s, w_ref[...], precision=_HI) + b_ref[...]


def _ada(cc, w, b):
    n = w.shape[1]
    bn = n // 4
    return pl.pallas_call(
        _ada_kernel,
        grid=(n // bn,),
        in_specs=[pl.BlockSpec((8, D_MODEL), lambda i: (0, 0)),
                  pl.BlockSpec((D_MODEL, bn), lambda i: (0, i)),
                  pl.BlockSpec((1, bn), lambda i: (0, i))],
        out_specs=pl.BlockSpec((8, bn), lambda i: (0, i)),
        out_shape=jax.ShapeDtypeStruct((8, n), F32),
        compiler_params=_params(("arbitrary",)),
        name="ada",
    )(cc, w, b)


def _proj_kernel(nw, period, x_ref, g_ref, sh_ref, sc_ref, conv_ref, *refs):
    x = x_ref[...]
    y = x * lax.rsqrt(jnp.mean(x * x, axis=-1, keepdims=True) + NORM_EPS)
    h = (y * g_ref[...]) * (1.0 + sc_ref[...]) + sh_ref[...]
    hb = h.astype(BF16)
    refs[nw][...] = _conv3_rows(_dot(hb, refs[0][...]), conv_ref[...], period)
    for w_ref, o_ref in zip(refs[1:nw], refs[nw + 1:]):
        o_ref[...] = _dot(hb, w_ref[...]).astype(o_ref.dtype)


def _proj(x, g, shift, scale, conv, period, weights, dtypes, tm):
    t = x.shape[0]
    row = lambda i: (i, 0)
    fix = lambda i: (0, 0)
    assert tm % period == 0
    in_specs = [pl.BlockSpec((tm, D_MODEL), row)] + [pl.BlockSpec((1, D_MODEL), fix)] * 3 + [pl.BlockSpec(conv.shape, fix)]
    in_specs += [pl.BlockSpec(w.shape, fix) for w in weights]
    return pl.pallas_call(
        functools.partial(_proj_kernel, len(weights), period),
        grid=(t // tm,),
        in_specs=in_specs,
        out_specs=[pl.BlockSpec((tm, w.shape[1]), row) for w in weights],
        out_shape=[jax.ShapeDtypeStruct((t, w.shape[1]), dt) for w, dt in zip(weights, dtypes)],
        compiler_params=_params(("parallel",)),
        name="proj",
    )(x, g, shift, scale, conv, *weights)


def _rwkv_prep(z, f_ref, w0_ref, wup_ref, a0_ref, aup_ref, kk_ref, ka_ref, gup_ref, dst):
    f = f_ref[...]
    r = f[:, 0:D_RWKV]
    k = f[:, D_RWKV:2 * D_RWKV]
    v = f[:, 2 * D_RWKV:3 * D_RWKV]
    o = 3 * D_RWKV
    wd = f[:, o:o + LORA_W]
    ad = f[:, o + LORA_W:o + LORA_W + LORA_A]
    gd = f[:, o + LORA_W + LORA_A:o + LORA_W + LORA_A + LORA_G]
    lw = w0_ref[z:z + 1, :] + _dot(jnp.tanh(wd).astype(BF16), wup_ref[z])
    w_log = -(jnp.maximum(-lw, 0.0) + jnp.log(1.0 + jnp.exp(-jnp.abs(lw)))) - 0.5
    a = _sigmoid(a0_ref[z:z + 1, :] + _dot(ad.astype(BF16), aup_ref[z]))
    g = _dot(_sigmoid(gd).astype(BF16), gup_ref[...])
    logw_s, kraw_s, a_s, keff_s, v_s, r_s, g_s = dst
    logw_s[z] = -jnp.exp(w_log)
    kraw_s[z] = k * kk_ref[...]
    a_s[z] = a
    keff_s[z] = k * (1.0 + (a - 1.0) * ka_ref[...])
    v_s[z] = v
    r_s[z] = r
    g_s[z] = g


def _rwkv_setup(units, masks):
    def st(x, lane_lo):
        return jnp.concatenate([jnp.where(lane_lo, x, 0.0), jnp.where(lane_lo, 0.0, x)], axis=0)

    def mm(p, q):
        return _dot(p.astype(BF16), q.astype(BF16))

    def each(f, *lists):
        return [f(*args) for args in zip(*lists)]

    def bcast(col):
        return jnp.broadcast_to(col, (LANES, LANES))

    lo = [masks[u[0]][0] for u in units]
    kr = [st(u[2], m) for u, m in zip(units, lo)]
    ssq = [jnp.sum(x * x, axis=1, keepdims=True) for x in kr]
    bon = [jnp.sum(st(u[6] * u[4] * u[7], m), axis=1, keepdims=True) for u, m in zip(units, lo)]
    inv = [bcast(1.0 / jnp.maximum(jnp.sqrt(x), 1e-12)) for x in ssq]
    bon = [bcast(x) for x in bon]
    pre = []
    for (z, lw, kraw, a, keff, v, r, rk), lane_lo, iv, bo in zip(units, lo, inv, bon):
        rev = z == 1
        cum = _scan_rows(lw, rev)
        tot = cum[0:1, :] if rev else cum[CHUNK - 1:CHUNK, :]
        e_in = jnp.exp(cum)
        e_ex = jnp.exp(cum - lw)
        e_ng = jnp.exp(-cum)
        e_rem = jnp.exp(tot - cum)
        ka = kraw * a
        vs = st(v, lane_lo)
        pre.append(dict(
            z=z, xk=(st(kraw * e_ex, lane_lo) * iv).astype(BF16), xr=st(r * e_in, lane_lo).astype(BF16),
            yk=st(keff * e_ng, lane_lo).astype(BF16), yb=(st(ka * e_ng, lane_lo) * iv).astype(BF16),
            ykg=st(keff * e_rem, lane_lo).astype(BF16), ybg=(st(ka * e_rem, lane_lo) * iv).astype(BF16),
            vs=vs, vb=vs.astype(BF16), gam=jnp.exp(tot), bonus=bo * vs))
    zs = [p["z"] for p in pre]

    m = [_dot_nt(jnp.concatenate([p["xk"], p["xr"]], axis=0), jnp.concatenate([p["yk"], p["yb"]], axis=0)) for p in pre]
    akk = [jnp.where(masks[z][2], x[:LANES, :LANES], 0.0).astype(BF16) for z, x in zip(zs, m)]
    ark = [jnp.where(masks[z][3], x[LANES:, :LANES], 0.0).astype(BF16) for z, x in zip(zs, m)]
    arb = [jnp.where(masks[z][3], x[LANES:, LANES:], 0.0).astype(BF16) for z, x in zip(zs, m)]
    a_d = [jnp.where(masks[z][2] & masks[z][4], x[:LANES, LANES:], 0.0) for z, x in zip(zs, m)]
    a_off = [jnp.where(masks[z][2] & jnp.logical_not(masks[z][4]), x[:LANES, LANES:], 0.0) for z, x in zip(zs, m)]
    eye = masks[0][5]

    a2 = each(mm, a_d, a_d)
    t_d = [eye - x for x in a_d]
    avk = [_dot(k, p["vb"]) for k, p in zip(akk, pre)]
    a4 = each(mm, a2, a2)
    t_d = each(lambda t, x: t + mm(t, x), t_d, a2)
    avr = [_dot(k, p["vb"]) for k, p in zip(ark, pre)]
    a8 = each(mm, a4, a4)
    t_d = each(lambda t, x: t + mm(t, x), t_d, a4)
    ds0 = [_dot_tn(p["vb"], p["ykg"]) for p in pre]
    t_d = each(lambda t, x: t + mm(t, x), t_d, a8)
    n1 = each(mm, t_d, a_off)
    n2 = each(mm, n1, n1)
    n3 = each(mm, n1, n2)
    tmat = each(lambda x1, x2, x3, t: mm(eye - x1 + x2 - x3, t).astype(BF16), n1, n2, n3, t_d)
    return [dict(z=p["z"], xk=p["xk"], xr=p["xr"], ybg=p["ybg"], vs=p["vs"], gam=p["gam"], bonus=p["bonus"],
                 tmat=t, avk=k, avr=r, arb=b, ds0=d)
            for p, t, k, r, b, d in zip(pre, tmat, avk, avr, arb, ds0)]


def _rwkv_advance(pre, s_prev, lnw, lnb, masks):
    sb = [s.astype(BF16) for s in s_prev]
    rhs = [_dot_nt(p["xk"], s) + p["avk"] for p, s in zip(pre, sb)]
    osr = [_dot_nt(p["xr"], s) + p["avr"] for p, s in zip(pre, sb)]
    ub = [_dot(p["tmat"], x.astype(BF16)).astype(BF16) for p, x in zip(pre, rhs)]
    o = [x - _dot(p["arb"], u) for x, p, u in zip(osr, pre, ub)]
    s_new = [s * p["gam"] + p["ds0"] - _dot_tn(u, p["ybg"]) for s, p, u in zip(s_prev, pre, ub)]
    full = (LANES, LANES)
    own = [masks[p["z"]][1] for p in pre]
    mu = [jnp.broadcast_to(jnp.sum(x, axis=1, keepdims=True) * (1.0 / RWKV_HEAD), full) for x in o]
    cen = [jnp.where(m, x - y, 0.0) for m, x, y in zip(own, o, mu)]
    var = [jnp.sum(x * x, axis=1, keepdims=True) * (1.0 / RWKV_HEAD) for x in cen]
    rstd = [jnp.broadcast_to(lax.rsqrt(x + GN_EPS), full) for x in var]
    y = [c * r * w + jnp.where(m, b, 0.0) + p["bonus"] for c, r, w, m, b, p in zip(cen, rstd, lnw, own, lnb, pre)]
    return [x[:CHUNK] + x[CHUNK:] for x in y], s_new


def _rwkv_masks():
    i = lax.broadcasted_iota(jnp.int32, (LANES, LANES), 0)
    j = lax.broadcasted_iota(jnp.int32, (LANES, LANES), 1)
    lane_lo = lax.broadcasted_iota(jnp.int32, (CHUNK, LANES), 1) < RWKV_HEAD
    same = (i // CHUNK) == (j // CHUNK)
    eye = jnp.where(i == j, 1.0, 0.0).astype(F32)
    diag16 = (i // 16) == (j // 16)
    out = []
    for z in range(N_DIR):
        before = (j % CHUNK) > (i % CHUNK) if z == 1 else (j % CHUNK) < (i % CHUNK)
        strict = same & before
        incl = same & (before | (i == j))
        out.append((lane_lo, same, strict, incl, diag16, eye))
    return out


def _rwkv_kernel(nch, f0_ref, f1_ref, w0_ref, wup_ref, a0_ref, aup_ref, kk_ref, ka_ref, rk_ref,
                 lnw_ref, lnb_ref, gup_ref, sin_ref, y0_ref, y1_ref, sout_ref,
                 s_ref, logw_s, kraw_s, a_s, keff_s, v_s, r_s, g_s):
    step = pl.program_id(0)

    @pl.when(step == 0)
    def _():
        s_ref[...] = sin_ref[...]

    dst = (logw_s, kraw_s, a_s, keff_s, v_s, r_s, g_s)
    for z, f_ref in ((0, f0_ref), (1, f1_ref)):
        _rwkv_prep(z, f_ref, w0_ref, wup_ref, a0_ref, aup_ref, kk_ref, ka_ref, gup_ref, dst)

    masks = _rwkv_masks()
    y_refs = (y0_ref, y1_ref)

    units, where = [], []
    for ci in range(nch):
        for z in range(N_DIR):
            r0 = (nch - 1 - ci if z == 1 else ci) * CHUNK
            rows = slice(r0, r0 + CHUNK)
            for p in range(RWKV_PAIRS):
                ls = slice(p * LANES, (p + 1) * LANES)
                units.append((z, logw_s[z, rows, ls], kraw_s[z, rows, ls], a_s[z, rows, ls], keff_s[z, rows, ls],
                              v_s[z, rows, ls], r_s[z, rows, ls], rk_ref[:, ls]))
                where.append((z, p, rows, ls))
    pre = _rwkv_setup(units, masks)
    per = N_DIR * RWKV_PAIRS
    state = [s_ref[z, p] for z in range(N_DIR) for p in range(RWKV_PAIRS)]
    lnw = [lnw_ref[:, p * LANES:(p + 1) * LANES] for z in range(N_DIR) for p in range(RWKV_PAIRS)]
    lnb = [lnb_ref[:, p * LANES:(p + 1) * LANES] for z in range(N_DIR) for p in range(RWKV_PAIRS)]
    for ci in range(nch):
        ys, state = _rwkv_advance(pre[ci * per:(ci + 1) * per], state, lnw, lnb, masks)
        for (z, p, rows, ls), y in zip(where[ci * per:(ci + 1) * per], ys):
            y_refs[z][rows, ls] = (y * g_s[z, rows, ls]).astype(BF16)
    for i, s_new in enumerate(state):
        s_ref[i // RWKV_PAIRS, i % RWKV_PAIRS] = s_new

    @pl.when(step == pl.num_programs(0) - 1)
    def _():
        sout_ref[...] = s_ref[...]


def _rwkv(feat, prm, s_init, tb):
    t = feat.shape[0]
    nb = t // tb
    nch = tb // CHUNK
    fix2 = lambda i: (0, 0)
    fix3 = lambda i: (0, 0, 0)
    fix4 = lambda i: (0, 0, 0, 0)
    fwd = lambda i: (i, 0)
    bwd = lambda i: (nb - 1 - i, 0)
    vec = pl.BlockSpec((1, D_RWKV), fix2)
    state = pl.BlockSpec((N_DIR, RWKV_PAIRS, LANES, LANES), fix4)
    big = pltpu.VMEM((N_DIR, tb, D_RWKV), F32)
    return pl.pallas_call(
        functools.partial(_rwkv_kernel, nch),
        grid=(nb,),
        in_specs=[pl.BlockSpec((tb, RWKV_COLS), fwd), pl.BlockSpec((tb, RWKV_COLS), bwd),
                  pl.BlockSpec((N_DIR, D_RWKV), fix2), pl.BlockSpec((N_DIR, LORA_W, D_RWKV), fix3),
                  pl.BlockSpec((N_DIR, D_RWKV), fix2), pl.BlockSpec((N_DIR, LORA_A, D_RWKV), fix3),
                  vec, vec, vec, vec, vec,
                  pl.BlockSpec((LORA_G, D_RWKV), fix2), state],
        out_specs=[pl.BlockSpec((tb, D_RWKV), fwd), pl.BlockSpec((tb, D_RWKV), bwd), state],
        out_shape=[jax.ShapeDtypeStruct((t, D_RWKV), BF16), jax.ShapeDtypeStruct((t, D_RWKV), BF16),
                   jax.ShapeDtypeStruct((N_DIR, RWKV_PAIRS, LANES, LANES), F32)],
        scratch_shapes=[pltpu.VMEM((N_DIR, RWKV_PAIRS, LANES, LANES), F32)] + [big] * 7,
        compiler_params=_params(("arbitrary",)),
        name="rwkv",
    )(feat, feat, prm["w0"], prm["w_up"], prm["a0"], prm["a_up"], prm["k_k"], prm["k_a"], prm["r_k"],
      prm["ln_w"], prm["ln_b"], prm["g_up"], s_init)


def _cummax_rows(x, reverse):
    n = x.shape[0]
    row = lax.broadcasted_iota(jnp.int32, x.shape, 0)
    d = 1
    while d < n:
        if reverse:
            x = jnp.maximum(x, jnp.where(row < n - d, pltpu.roll(x, n - d, axis=0), -jnp.inf))
        else:
            x = jnp.maximum(x, jnp.where(row >= d, pltpu.roll(x, d, axis=0), -jnp.inf))
        d *= 2
    return x


def _mlstm_kernel(nch, qk0_ref, qk1_ref, mv0_ref, mv1_ref, if0_ref, if1_ref, conv_ref, bias_ref, ng_ref,
                  cin_ref, min_ref, h0_ref, h1_ref, cout_ref, mout_ref,
                  c_s, m_s, qk_s, gi_s, lf_s):
    step = pl.program_id(0)
    rows_n = qk0_ref.shape[0]
    nh = MLSTM_HEADS

    @pl.when(step == 0)
    def _():
        c_s[...] = cin_ref[...]
        m_s[...] = min_ref[...]

    for z, (qk_ref, if_ref) in enumerate(((qk0_ref, if0_ref), (qk1_ref, if1_ref))):
        qk = _conv3_rows(qk_ref[...], conv_ref[...], rows_n)
        qk_s[z] = qk * _sigmoid(qk)
        gate = if_ref[...] + bias_ref[...]
        gi_s[z] = gate[:, :LANES]
        fg = gate[:, LANES:]
        lf_s[z] = jnp.minimum(fg, 0.0) - jnp.log(1.0 + jnp.exp(-jnp.abs(fg)))

    ti = lax.broadcasted_iota(jnp.int32, (CHUNK, CHUNK), 0)
    si = lax.broadcasted_iota(jnp.int32, (CHUNK, CHUNK), 1)
    causal = (si <= ti, si >= ti)
    lane = lax.broadcasted_iota(jnp.int32, (1, LANES), 1)
    mv_refs = (mv0_ref, mv1_ref)
    h_refs = (h0_ref, h1_ref)
    ones = jnp.ones((CHUNK, LANES), BF16)
    lane0 = lax.broadcasted_iota(jnp.int32, (CHUNK, LANES), 1) == 0

    mrow = m_s[0:1, :]
    us = []
    for ci in range(nch):
        m_next = mrow
        for z in range(N_DIR):
            rev = z == 1
            r0 = (nch - 1 - ci if rev else ci) * CHUNK
            rows = slice(r0, r0 + CHUNK)
            gi = gi_s[z, rows, :]
            b = _scan_rows(lf_s[z, rows, :], rev)
            d = gi - b
            bend = b[0:1, :] if rev else b[CHUNK - 1:CHUNK, :]
            m_row = b + jnp.maximum(mrow, _cummax_rows(d, rev))
            a_int = jnp.exp(b + mrow - m_row)
            c1 = b - m_row
            g_end = bend - b + gi
            m_new = jnp.maximum(bend + mrow, jnp.max(g_end, axis=0, keepdims=True))
            w_end = jnp.exp(g_end - m_new)
            keep = jnp.exp(bend + mrow - m_new)
            unit_lanes = (lane >= z * nh) & (lane < (z + 1) * nh)
            m_next = jnp.where(unit_lanes, m_new, m_next)
            d_t = d.T
            for h in range(nh):
                u = z * nh + h
                us.append(dict(
                    ci=ci, z=z, h=h, u=u, rows=rows, c1=c1[:, u:u + 1], drow=d_t[u:u + 1, :], a_int=a_int[:, u:u + 1],
                    w_end=w_end[:, u:u + 1], keep=keep[:, u:u + 1], einv=jnp.exp(-m_row[:, u:u + 1]),
                    q=qk_s[z, rows, h * MLSTM_DQK:(h + 1) * MLSTM_DQK],
                    k=qk_s[z, rows, D_MLSTM_QK + h * MLSTM_DQK:D_MLSTM_QK + (h + 1) * MLSTM_DQK] * (MLSTM_DQK ** -0.5),
                    v=mv_refs[z][rows, h * MLSTM_DV:(h + 1) * MLSTM_DV]))
        mrow = m_next
    def col(key, width):
        return [jnp.broadcast_to(x[key], (CHUNK, width)) for x in us]

    qb = [x["q"].astype(BF16) for x in us]
    kb = [x["k"].astype(BF16) for x in us]
    qk = [_dot_nt(x, y) for x, y in zip(qb, kb)]
    web = col("w_end", MLSTM_DV)
    dc = [_dot_tn(x, jnp.concatenate([(y["v"] * w).astype(BF16), jnp.where(lane0, w, 0.0).astype(BF16)], axis=1))
          for x, y, w in zip(kb, us, web)]
    c1b = col("c1", CHUNK)
    smat = [x * jnp.exp(jnp.where(causal[y["z"]], c + y["drow"], -jnp.inf)) for x, y, c in zip(qk, us, c1b)]
    sv = [_dot(x.astype(BF16), jnp.concatenate([y["v"].astype(BF16), ones], axis=1)) for x, y in zip(smat, us)]
    keepb = col("keep", 2 * MLSTM_DV)
    ct = [c_s[u] for u in range(N_DIR * nh)]
    qc = []
    for i, x in enumerate(us):
        qc.append(_dot(qb[i], ct[x["u"]].astype(BF16)))
        ct[x["u"]] = keepb[i] * ct[x["u"]] + dc[i]
    aib = col("a_int", 2 * MLSTM_DV)
    tot = [y + a * w for y, a, w in zip(sv, aib, qc)]
    rden = [1.0 / jnp.maximum(jnp.abs(x[:, MLSTM_DV:MLSTM_DV + 1]), y["einv"]) for x, y in zip(tot, us)]
    rden = [jnp.broadcast_to(x, (CHUNK, MLSTM_DV)) for x in rden]
    hh = [x[:, :MLSTM_DV] * r for x, r in zip(tot, rden)]
    hms = [jnp.mean(x * x, axis=1, keepdims=True) for x in hh]
    rstd = [jnp.broadcast_to(lax.rsqrt(x + NORM_EPS), (CHUNK, MLSTM_DV)) for x in hms]
    for x, y, r in zip(us, hh, rstd):
        cols = slice(x["h"] * MLSTM_DV, (x["h"] + 1) * MLSTM_DV)
        h_refs[x["z"]][x["rows"], cols] = (y * r * ng_ref[:, cols]).astype(BF16)
    for u, c in enumerate(ct):
        c_s[u] = c
    m_s[...] = jnp.broadcast_to(mrow, m_s.shape)

    @pl.when(step == pl.num_programs(0) - 1)
    def _():
        cout_ref[...] = c_s[...]
        mout_ref[...] = m_s[...]


def _mlstm(qk, mv, gates, prm, state, rows, ncol):
    nch = rows // CHUNK
    nu = N_DIR * MLSTM_HEADS
    qk2 = qk.reshape(rows, ncol * 2 * D_MLSTM_QK)
    mv2 = mv.reshape(rows, ncol * D_MLSTM)
    g2 = gates.reshape(rows, ncol * 2 * LANES)
    fwd = lambda i: (0, i)
    bwd = lambda i: (0, ncol - 1 - i)
    fix2 = lambda i: (0, 0)
    fix3 = lambda i: (0, 0, 0)
    cspec = pl.BlockSpec((nu, MLSTM_DQK, 2 * MLSTM_DV), fix3)
    mspec = pl.BlockSpec((8, LANES), fix2)
    wide = lambda m: pl.BlockSpec((rows, 2 * D_MLSTM_QK), m)
    outs = pl.pallas_call(
        functools.partial(_mlstm_kernel, nch),
        grid=(ncol,),
        in_specs=[wide(fwd), wide(bwd), wide(fwd), wide(bwd),
                  pl.BlockSpec((rows, 2 * LANES), fwd), pl.BlockSpec((rows, 2 * LANES), bwd),
                  pl.BlockSpec((3, 2 * D_MLSTM_QK), fix2), pl.BlockSpec((1, 2 * LANES), fix2),
                  pl.BlockSpec((1, D_MLSTM), fix2), cspec, mspec],
        out_specs=[wide(fwd), wide(bwd), cspec, mspec],
        out_shape=[jax.ShapeDtypeStruct((rows, ncol * D_MLSTM), BF16)] * 2
        + [jax.ShapeDtypeStruct((nu, MLSTM_DQK, 2 * MLSTM_DV), F32), jax.ShapeDtypeStruct((8, LANES), F32)],
        scratch_shapes=[pltpu.VMEM((nu, MLSTM_DQK, 2 * MLSTM_DV), F32), pltpu.VMEM((8, LANES), F32),
                        pltpu.VMEM((N_DIR, rows, 2 * D_MLSTM_QK), F32),
                        pltpu.VMEM((N_DIR, rows, LANES), F32), pltpu.VMEM((N_DIR, rows, LANES), F32)],
        compiler_params=_params(("arbitrary",)),
        name="mlstm",
    )(qk2, qk2, mv2, mv2, g2, g2, prm["conv"], prm["bias"], prm["norm_g"], *state)
    h0, h1 = outs[0].reshape(rows * ncol, D_MLSTM), outs[1].reshape(rows * ncol, D_MLSTM)
    return h0, h1, tuple(outs[2:])


def _merge_kernel(x_ref, ya0_ref, ya1_ref, hb0_ref, hb1_ref, po_ref, ga_ref, gb_ref, wa_ref, wb_ref, wo_ref,
                  gt1_ref, g2_ref, sh2_ref, sc2_ref, wq_ref, x1_ref, h2_ref, q_ref):
    f32 = lambda ref: ref[...].astype(F32)
    ya = f32(ya0_ref) + f32(ya1_ref)
    yb = (f32(hb0_ref) + f32(hb1_ref)) * _sigmoid(f32(po_ref))
    merged = (_sigmoid(f32(ga_ref)) * _dot(ya.astype(BF16), wa_ref[...])
              + _sigmoid(f32(gb_ref)) * _dot(yb.astype(BF16), wb_ref[...]))
    x1 = x_ref[...] + gt1_ref[...] * _dot(merged.astype(BF16), wo_ref[...])
    x1_ref[...] = x1
    y = x1 * lax.rsqrt(jnp.mean(x1 * x1, axis=-1, keepdims=True) + NORM_EPS)
    h2 = ((y * g2_ref[...]) * (1.0 + sc2_ref[...]) + sh2_ref[...]).astype(BF16)
    h2_ref[...] = h2
    q_ref[...] = _dot(h2, wq_ref[...]).astype(q_ref.dtype)


def _merge(x, ya0, ya1, hb0, hb1, po, ga, gb, wa, wb, wo, gt1, g2, sh2, sc2, wq, tm):
    t = x.shape[0]
    row = lambda i: (i, 0)
    fix = lambda i: (0, 0)
    rs = lambda n: pl.BlockSpec((tm, n), row)
    vec = pl.BlockSpec((1, D_MODEL), fix)
    nq = wq.shape[1]
    return pl.pallas_call(
        _merge_kernel,
        grid=(t // tm,),
        in_specs=[rs(D_MODEL), rs(D_RWKV), rs(D_RWKV), rs(D_MLSTM), rs(D_MLSTM), rs(D_MLSTM), rs(D_MODEL), rs(D_MODEL),
                  pl.BlockSpec(wa.shape, fix), pl.BlockSpec(wb.shape, fix), pl.BlockSpec(wo.shape, fix),
                  vec, vec, vec, vec, pl.BlockSpec(wq.shape, fix)],
        out_specs=[rs(D_MODEL), rs(D_MODEL), rs(nq)],
        out_shape=[jax.ShapeDtypeStruct((t, D_MODEL), F32), jax.ShapeDtypeStruct((t, D_MODEL), BF16),
                   jax.ShapeDtypeStruct((t, nq), BF16)],
        compiler_params=_params(("parallel",)),
        name="merge",
    )(x, ya0, ya1, hb0, hb1, po, ga, gb, wa, wb, wo, gt1, g2, sh2, sc2, wq)


def _sort_pairs(n):
    pairs = []
    t = max(1, (n - 1).bit_length())
    p = 1 << (t - 1)
    while p > 0:
        q, r, d = 1 << (t - 1), 0, p
        while d > 0:
            pairs += [(i, i + d) for i in range(n - d) if (i & p) == r]
            d, q, r = q - p, q >> 1, p
        p >>= 1
    return pairs


def _bitonic_desc(c):
    n = len(c)
    d = n // 2
    while d > 0:
        for i in range(n):
            if (i & d) == 0:
                c[i], c[i + d] = jnp.maximum(c[i], c[i + d]), jnp.minimum(c[i], c[i + d])
        d //= 2
    return c


def _merge_top(x, y):
    n = len(x)
    return _bitonic_desc([jnp.maximum(x[i], y[n - 1 - i]) for i in range(n)])


def _top16_levels(scores):
    x = [scores[8 * i:8 * (i + 1), :] for i in range(N_KEYS // 8)]
    for i, j in _sort_pairs(len(x)):
        x[i], x[j] = jnp.maximum(x[i], x[j]), jnp.minimum(x[i], x[j])
    for shift in (4, 2, 1):
        x = _merge_top(x, [pltpu.roll(v, shift, axis=0) for v in x])
    return x


def _router_kernel(q_ref, keys_ref, r2_ref, e2_ref, n1_ref, e1_ref):
    qb = q_ref[...]
    s1 = _dot_nt(keys_ref[0, 0], qb[:, :KEY_DIM])
    s2 = _dot_nt(keys_ref[0, 1], qb[:, KEY_DIM:])
    a = _top16_levels(s1)
    b = _top16_levels(s2)
    k = PEER_TOPK
    nj = [k // (i + 1) for i in range(k)]
    cell = {(i, j): a[i] + b[j] for i in range(k) for j in range(nj[i])}
    ninf = jnp.full(a[0].shape, -jnp.inf, F32)
    pad = lambda lst: lst + [ninf] * (k - len(lst))
    lists = [pad([cell[(i, j)] for j in range(nj[i])]) for i in range(4)]
    lists += [pad([cell[(i, j)] for i in range(4, k) if j < nj[i]]) for j in range(3)]
    top = lists[0]
    for other in lists[1:]:
        top = _merge_top(top, other)
    tau = top[k - 1]
    mx = cell[(0, 0)]
    zsum = jnp.zeros_like(tau)
    n_rank = []
    for i in range(k):
        cnt = jnp.zeros_like(tau)
        for j in range(nj[i]):
            sel = cell[(i, j)] >= tau
            zsum = zsum + jnp.where(sel, jnp.exp(cell[(i, j)] - mx), 0.0)
            cnt = cnt + jnp.where(sel, 1.0, 0.0)
        n_rank.append(cnt)
    rz = 1.0 / zsum
    for blk in range(N_KEYS // 16):
        r2, e2 = [], []
        for half in range(2):
            rows = slice(16 * blk + 8 * half, 16 * blk + 8 * (half + 1))
            s1b, s2b = s1[rows, :], s2[rows, :]
            n1 = jnp.zeros_like(s1b)
            for i in reversed(range(k)):
                n1 = jnp.where(s1b == a[i], n_rank[i], n1)
            n1_ref[0, rows, :] = n1
            e1_ref[0, rows, :] = jnp.exp(s1b - a[0]) * rz
            rank = jnp.zeros_like(s2b)
            for lvl in b:
                rank = rank + jnp.where(lvl > s2b, 1.0, 0.0)
            r2.append(rank)
            e2.append(jnp.exp(s2b - b[0]))
        rows = slice(16 * blk, 16 * (blk + 1))
        r2_ref[0, rows, :] = jnp.concatenate(r2, axis=0).astype(BF16)
        e2_ref[0, rows, :] = jnp.concatenate(e2, axis=0).astype(BF16)


def _router(q, keys, tt):
    t = q.shape[0]
    spec = pl.BlockSpec((1, N_KEYS, tt), lambda i, h: (h, 0, i))
    shape = lambda dt: jax.ShapeDtypeStruct((PEER_HEADS, N_KEYS, t), dt)
    return pl.pallas_call(
        _router_kernel,
        grid=(t // tt, PEER_HEADS),
        in_specs=[pl.BlockSpec((tt, PEER_QDIM), lambda i, h: (i, h)),
                  pl.BlockSpec((1, 2, N_KEYS, KEY_DIM), lambda i, h: (h, 0, 0, 0))],
        out_specs=[spec] * 4,
        out_shape=[shape(BF16), shape(BF16), shape(F32), shape(F32)],
        compiler_params=_params(("parallel", "parallel")),
        name="router",
    )(q, keys)


EXPERT_SUB = 512
TILE16 = 16
GATE_TOKENS = 512
GATE_TILES = 4


def _experts_kernel(nsub, h2_ref, u_ref, vt_ref, r2_ref, e2_ref, n1_ref, e1_ref, x1_ref, gt2_ref, fg_ref, o_ref,
                    acc_ref, act_ref, w_ref):
    e = pl.program_id(1)
    na = EXPERT_SUB // N_KEYS
    tt = h2_ref.shape[0]

    @pl.when(e == 0)
    def _():
        acc_ref[...] = jnp.zeros_like(acc_ref)

    def first_dot(si, dst):
        act_ref[dst] = _dot_nt(u_ref[si * EXPERT_SUB:(si + 1) * EXPERT_SUB, :], h2_ref[...])

    def build(si, cur):
        for ai in range(na):
            a = si * na + ai
            rows_a = slice(ai * N_KEYS, (ai + 1) * N_KEYS)
            for tb in range(tt // GATE_TOKENS):
                tok = slice(tb * GATE_TOKENS, (tb + 1) * GATE_TOKENS)
                gate = []
                for g0 in range(0, N_KEYS // TILE16, GATE_TILES):
                    part = [jnp.zeros((TILE16, GATE_TOKENS), BF16)] * GATE_TILES
                    for h in range(PEER_HEADS):
                        n1 = jnp.broadcast_to(n1_ref[h, a:a + 1, tok], (TILE16, GATE_TOKENS)).astype(BF16)
                        e1 = jnp.broadcast_to(e1_ref[h, a:a + 1, tok], (TILE16, GATE_TOKENS)).astype(BF16)
                        for k in range(GATE_TILES):
                            rows = slice((g0 + k) * TILE16, (g0 + k + 1) * TILE16)
                            zero = jnp.zeros((TILE16, GATE_TOKENS), BF16)
                            part[k] = part[k] + jnp.where(r2_ref[h, rows, tok] < n1, e2_ref[h, rows, tok], zero) * e1
                    gate += part
                act = act_ref[cur, rows_a, tok]
                gl = (0.5 * act * (1.0 + lax.erf(act * (2.0 ** -0.5)))).astype(BF16)
                w_ref[cur, rows_a, tok] = jnp.concatenate(gate, axis=0) * gl

    first_dot(0, 0)
    total = None
    for si in range(nsub):
        cur = si % 2
        if si + 1 < nsub:
            first_dot(si + 1, 1 - cur)
        build(si, cur)
        part = _dot(vt_ref[si], w_ref[cur])
        total = part if total is None else total + part
    acc_ref[...] += total

    @pl.when(e == pl.num_programs(1) - 1)
    def _():
        x2 = x1_ref[...] + gt2_ref[...] * acc_ref[...].T
        y = x2 * lax.rsqrt(jnp.mean(x2 * x2, axis=-1, keepdims=True) + NORM_EPS)
        o_ref[...] = y * fg_ref[...]


def _experts(h2, u, vt, r2, e2, n1, e1, x1, gt2, fg, tt, et):
    t = h2.shape[0]
    ne = u.shape[0]
    tok = lambda i, e: (i, 0)
    fix = lambda i, e: (0, 0)
    rt = pl.BlockSpec((PEER_HEADS, N_KEYS, tt), lambda i, e: (0, 0, i))
    rs = pl.BlockSpec((PEER_HEADS, et // N_KEYS, tt), lambda i, e: (0, e, i))
    return pl.pallas_call(
        functools.partial(_experts_kernel, et // EXPERT_SUB),
        grid=(t // tt, ne // et),
        in_specs=[pl.BlockSpec((tt, D_MODEL), tok),
                  pl.BlockSpec((et, D_MODEL), lambda i, e: (e, 0)),
                  pl.BlockSpec((et // EXPERT_SUB, D_MODEL, EXPERT_SUB), lambda i, e: (e, 0, 0)),
                  rt, rt, rs, rs,
                  pl.BlockSpec((tt, D_MODEL), tok),
                  pl.BlockSpec((1, D_MODEL), fix), pl.BlockSpec((1, D_MODEL), fix)],
        out_specs=pl.BlockSpec((tt, D_MODEL), tok),
        out_shape=jax.ShapeDtypeStruct((t, D_MODEL), F32),
        scratch_shapes=[pltpu.VMEM((D_MODEL, tt), F32), pltpu.VMEM((2, EXPERT_SUB, tt), F32),
                        pltpu.VMEM((2, EXPERT_SUB, tt), BF16)],
        compiler_params=_params(("parallel", "arbitrary")),
        name="experts",
    )(h2, u, vt, r2, e2, n1, e1, x1, gt2, fg)


def _tile(n, pref):
    return pref if n % pref == 0 else n


def kernel(x, c, ctx, c_ctx, ada_w, ada_b, norm1_g, w_in, rwkv_conv, rwkv_w0, rwkv_w_up, rwkv_a0, rwkv_a_up, rwkv_g_up, rwkv_k_k, rwkv_k_a, rwkv_r_k, rwkv_ln_w, rwkv_ln_b, mlstm_conv, mlstm_i_b, mlstm_f_b, mlstm_norm_g, w_branch_a, w_branch_b, w_out, norm2_g, peer_wq, peer_keys, peer_u, peer_v, final_g):
    assert x.shape[0] == 1 and ada_w.shape[0] == 1, "one layer, batch 1"
    t, tc = x.shape[1], ctx.shape[1]
    rows = t // GRID_W
    xs, cs = x[0], ctx[0]

    cc = jnp.zeros((8, D_MODEL), F32).at[0].set(c[0]).at[1].set(c_ctx)
    mods = _ada(cc, ada_w[0], ada_b[0][None])
    sh1, sc1, gt1, sh2, sc2, gt2 = [mods[0:1, i * D_MODEL:(i + 1) * D_MODEL] for i in range(N_MOD)]
    csh1, csc1 = mods[1:2, 0:D_MODEL], mods[1:2, D_MODEL:2 * D_MODEL]

    w = w_in[0]
    o = 0
    parts = []
    for n in (RWKV_COLS, 2 * D_MLSTM_QK, D_MLSTM, 2 * N_DIR * MLSTM_HEADS, D_MLSTM, D_MODEL, D_MODEL):
        parts.append(w[:, o:o + n])
        o += n
    ng = N_DIR * MLSTM_HEADS
    gpad = ((0, 0), (0, LANES - ng))
    parts[3] = jnp.concatenate([jnp.pad(parts[3][:, :ng], gpad), jnp.pad(parts[3][:, ng:], gpad)], axis=1)
    weights = [p.astype(BF16) for p in parts]
    out_dt = [F32, F32, BF16, F32, BF16, BF16, BF16]
    g1 = norm1_g[0][None]

    rw_prm = dict(w0=rwkv_w0[0], w_up=rwkv_w_up[0].astype(BF16), a0=rwkv_a0[0],
                  a_up=rwkv_a_up[0].astype(BF16), k_k=rwkv_k_k[0][None], k_a=rwkv_k_a[0][None], r_k=rwkv_r_k[0][None],
                  ln_w=rwkv_ln_w[0][None], ln_b=rwkv_ln_b[0][None], g_up=rwkv_g_up[0].astype(BF16))
    bias = jnp.concatenate([jnp.pad(mlstm_i_b[0].reshape(1, ng), gpad), jnp.pad(mlstm_f_b[0].reshape(1, ng), gpad)], axis=1)
    ml_prm = dict(conv=mlstm_conv[0], bias=bias, norm_g=mlstm_norm_g[0][None])

    p_rw, p_qk, p_mv, p_if, _, _, _ = _proj(cs, g1, csh1, csc1, rwkv_conv[0], tc, weights, out_dt, tc)
    s_zero = jnp.zeros((N_DIR, RWKV_PAIRS, LANES, LANES), F32)
    _, _, rw_state = _rwkv(p_rw, rw_prm, s_zero, tc)
    m_zero = (jnp.zeros((ng, MLSTM_DQK, 2 * MLSTM_DV), F32), jnp.zeros((8, LANES), F32))
    _, _, ml_state = _mlstm(p_qk, p_mv, p_if, ml_prm, m_zero, tc, 1)

    p_rw, p_qk, p_mv, p_if, p_o, p_ga, p_gb = _proj(xs, g1, sh1, sc1, rwkv_conv[0], GRID_W, weights, out_dt, _tile(t, 256))
    ya0, ya1, _ = _rwkv(p_rw, rw_prm, rw_state, _tile(t, 256))
    hb0, hb1, _ = _mlstm(p_qk, p_mv, p_if, ml_prm, ml_state, rows, GRID_W)

    x1, h2, q = _merge(xs, ya0, ya1, hb0, hb1, p_o, p_ga, p_gb, w_branch_a[0].astype(BF16), w_branch_b[0].astype(BF16),
                       w_out[0].astype(BF16), gt1, norm2_g[0][None], sh2, sc2, peer_wq[0].astype(BF16), _tile(t, 256))
    r2, e2, n1, e1 = _router(q, peer_keys[0].astype(BF16), _tile(t, 512))
    vt = peer_v[0].astype(BF16).reshape(-1, EXPERT_SUB, D_MODEL).transpose(0, 2, 1)
    out = _experts(h2, peer_u[0].astype(BF16), vt, r2, e2, n1, e1, x1, gt2, final_g[None],
                   _tile(t, 512), 2048)
    return out[None]
```

```python
import functools

import jax
import jax.numpy as jnp
from jax import lax
from jax.experimental import pallas as pl
from jax.experimental.pallas import tpu as pltpu

F32 = jnp.float32
BF16 = jnp.bfloat16

D_MODEL = 1024
GRID_W = 64
N_MOD = 6
NORM_EPS = 1e-6

RWKV_HEAD = 64
RWKV_HEADS = 8
D_RWKV = RWKV_HEADS * RWKV_HEAD
LORA_W = 64
LORA_A = 64
LORA_G = 128
RWKV_COLS = 3 * D_RWKV + LORA_W + LORA_A + LORA_G
GN_EPS = 64e-5
RWKV_PAIRS = RWKV_HEADS // 2

MLSTM_HEADS = 4
MLSTM_DQK = 64
MLSTM_DV = 128
D_MLSTM_QK = MLSTM_HEADS * MLSTM_DQK
D_MLSTM = MLSTM_HEADS * MLSTM_DV
N_DIR = 2
CHUNK = 64

PEER_HEADS = 8
N_KEYS = 128
PEER_TOPK = 16
KEY_DIM = 128
PEER_QDIM = 2 * KEY_DIM

LANES = 128
VMEM_LIMIT = 56 * 1024 * 1024

_HI = lax.Precision.HIGHEST


def _dot(a, b, precision=None):
    return jnp.dot(a, b, preferred_element_type=F32, precision=precision)


def _dot_nt(a, b):
    return lax.dot_general(a, b, (((1,), (1,)), ((), ())), preferred_element_type=F32)


def _dot_tn(a, b):
    return lax.dot_general(a, b, (((0,), (0,)), ((), ())), preferred_element_type=F32)


def _sigmoid(x):
    return 1.0 / (1.0 + jnp.exp(-x))


def _params(sem):
    return pltpu.CompilerParams(dimension_semantics=sem, vmem_limit_bytes=VMEM_LIMIT)


def _scan_rows(x, reverse):
    n = x.shape[0]
    row = lax.broadcasted_iota(jnp.int32, x.shape, 0)
    d = 1
    while d < n:
        if reverse:
            x = x + jnp.where(row < n - d, pltpu.roll(x, n - d, axis=0), 0.0)
        else:
            x = x + jnp.where(row >= d, pltpu.roll(x, d, axis=0), 0.0)
        d *= 2
    return x


def _conv3_rows(x, w, period):
    n = x.shape[0]
    pos = lax.broadcasted_iota(jnp.int32, x.shape, 0) % period
    prev = jnp.where(pos == 0, 0.0, pltpu.roll(x, 1, axis=0))
    nxt = jnp.where(pos == period - 1, 0.0, pltpu.roll(x, n - 1, axis=0))
    return w[0:1] * prev + w[1:2] * x + w[2:3] * nxt


def _ada_kernel(c_ref, w_ref, b_ref, o_ref):
    c = c_ref[...]
    s = c * _sigmoid(c)
    o_ref[...] = _dot(s, w_ref[...], precision=_HI) + b_ref[...]


def _ada(cc, w, b):
    n = w.shape[1]
    bn = n // 4
    return pl.pallas_call(
        _ada_kernel,
        grid=(n // bn,),
        in_specs=[pl.BlockSpec((8, D_MODEL), lambda i: (0, 0)),
                  pl.BlockSpec((D_MODEL, bn), lambda i: (0, i)),
                  pl.BlockSpec((1, bn), lambda i: (0, i))],
        out_specs=pl.BlockSpec((8, bn), lambda i: (0, i)),
        out_shape=jax.ShapeDtypeStruct((8, n), F32),
        compiler_params=_params(("arbitrary",)),
        name="ada",
    )(cc, w, b)


def _proj_kernel(nw, period, x_ref, g_ref, sh_ref, sc_ref, conv_ref, *refs):
    x = x_ref[...]
    y = x * lax.rsqrt(jnp.mean(x * x, axis=-1, keepdims=True) + NORM_EPS)
    h = (y * g_ref[...]) * (1.0 + sc_ref[...]) + sh_ref[...]
    hb = h.astype(BF16)
    refs[nw][...] = _conv3_rows(_dot(hb, refs[0][...]), conv_ref[...], period)
    for w_ref, o_ref in zip(refs[1:nw], refs[nw + 1:]):
        o_ref[...] = _dot(hb, w_ref[...]).astype(o_ref.dtype)


def _proj(x, g, shift, scale, conv, period, weights, dtypes, tm):
    t = x.shape[0]
    row = lambda i: (i, 0)
    fix = lambda i: (0, 0)
    assert tm % period == 0
    in_specs = [pl.BlockSpec((tm, D_MODEL), row)] + [pl.BlockSpec((1, D_MODEL), fix)] * 3 + [pl.BlockSpec(conv.shape, fix)]
    in_specs += [pl.BlockSpec(w.shape, fix) for w in weights]
    return pl.pallas_call(
        functools.partial(_proj_kernel, len(weights), period),
        grid=(t // tm,),
        in_specs=in_specs,
        out_specs=[pl.BlockSpec((tm, w.shape[1]), row) for w in weights],
        out_shape=[jax.ShapeDtypeStruct((t, w.shape[1]), dt) for w, dt in zip(weights, dtypes)],
        compiler_params=_params(("parallel",)),
        name="proj",
    )(x, g, shift, scale, conv, *weights)


def _rwkv_prep(z, f_ref, w0_ref, wup_ref, a0_ref, aup_ref, kk_ref, ka_ref, gup_ref, dst):
    f = f_ref[...]
    r = f[:, 0:D_RWKV]
    k = f[:, D_RWKV:2 * D_RWKV]
    v = f[:, 2 * D_RWKV:3 * D_RWKV]
    o = 3 * D_RWKV
    wd = f[:, o:o + LORA_W]
    ad = f[:, o + LORA_W:o + LORA_W + LORA_A]
    gd = f[:, o + LORA_W + LORA_A:o + LORA_W + LORA_A + LORA_G]
    lw = w0_ref[z:z + 1, :] + _dot(jnp.tanh(wd).astype(BF16), wup_ref[z])
    w_log = -(jnp.maximum(-lw, 0.0) + jnp.log(1.0 + jnp.exp(-jnp.abs(lw)))) - 0.5
    a = _sigmoid(a0_ref[z:z + 1, :] + _dot(ad.astype(BF16), aup_ref[z]))
    g = _dot(_sigmoid(gd).astype(BF16), gup_ref[...])
    logw_s, kraw_s, a_s, keff_s, v_s, r_s, g_s = dst
    logw_s[z] = -jnp.exp(w_log)
    kraw_s[z] = k * kk_ref[...]
    a_s[z] = a
    keff_s[z] = k * (1.0 + (a - 1.0) * ka_ref[...])
    v_s[z] = v
    r_s[z] = r
    g_s[z] = g


def _rwkv_setup(units, masks):
    def st(x, lane_lo):
        return jnp.concatenate([jnp.where(lane_lo, x, 0.0), jnp.where(lane_lo, 0.0, x)], axis=0)

    def mm(p, q):
        return _dot(p.astype(BF16), q.astype(BF16))

    def each(f, *lists):
        return [f(*args) for args in zip(*lists)]

    def bcast(col):
        return jnp.broadcast_to(col, (LANES, LANES))

    lo = [masks[u[0]][0] for u in units]
    kr = [st(u[2], m) for u, m in zip(units, lo)]
    ssq = [jnp.sum(x * x, axis=1, keepdims=True) for x in kr]
    bon = [jnp.sum(st(u[6] * u[4] * u[7], m), axis=1, keepdims=True) for u, m in zip(units, lo)]
    inv = [bcast(1.0 / jnp.maximum(jnp.sqrt(x), 1e-12)) for x in ssq]
    bon = [bcast(x) for x in bon]
    pre = []
    for (z, lw, kraw, a, keff, v, r, rk), lane_lo, iv, bo in zip(units, lo, inv, bon):
        rev = z == 1
        cum = _scan_rows(lw, rev)
        tot = cum[0:1, :] if rev else cum[CHUNK - 1:CHUNK, :]
        e_in = jnp.exp(cum)
        e_ex = jnp.exp(cum - lw)
        e_ng = jnp.exp(-cum)
        e_rem = jnp.exp(tot - cum)
        ka = kraw * a
        vs = st(v, lane_lo)
        pre.append(dict(
            z=z, xk=(st(kraw * e_ex, lane_lo) * iv).astype(BF16), xr=st(r * e_in, lane_lo).astype(BF16),
            yk=st(keff * e_ng, lane_lo).astype(BF16), yb=(st(ka * e_ng, lane_lo) * iv).astype(BF16),
            ykg=st(keff * e_rem, lane_lo).astype(BF16), ybg=(st(ka * e_rem, lane_lo) * iv).astype(BF16),
            vs=vs, vb=vs.astype(BF16), gam=jnp.exp(tot), bonus=bo * vs))
    zs = [p["z"] for p in pre]

    m = [_dot_nt(jnp.concatenate([p["xk"], p["xr"]], axis=0), jnp.concatenate([p["yk"], p["yb"]], axis=0)) for p in pre]
    akk = [jnp.where(masks[z][2], x[:LANES, :LANES], 0.0).astype(BF16) for z, x in zip(zs, m)]
    ark = [jnp.where(masks[z][3], x[LANES:, :LANES], 0.0).astype(BF16) for z, x in zip(zs, m)]
    arb = [jnp.where(masks[z][3], x[LANES:, LANES:], 0.0).astype(BF16) for z, x in zip(zs, m)]
    a_d = [jnp.where(masks[z][2] & masks[z][4], x[:LANES, LANES:], 0.0) for z, x in zip(zs, m)]
    a_off = [jnp.where(masks[z][2] & jnp.logical_not(masks[z][4]), x[:LANES, LANES:], 0.0) for z, x in zip(zs, m)]
    eye = masks[0][5]

    a2 = each(mm, a_d, a_d)
    t_d = [eye - x for x in a_d]
    avk = [_dot(k, p["vb"]) for k, p in zip(akk, pre)]
    a4 = each(mm, a2, a2)
    t_d = each(lambda t, x: t + mm(t, x), t_d, a2)
    avr = [_dot(k, p["vb"]) for k, p in zip(ark, pre)]
    a8 = each(mm, a4, a4)
    t_d = each(lambda t, x: t + mm(t, x), t_d, a4)
    ds0 = [_dot_tn(p["vb"], p["ykg"]) for p in pre]
    t_d = each(lambda t, x: t + mm(t, x), t_d, a8)
    n1 = each(mm, t_d, a_off)
    n2 = each(mm, n1, n1)
    n3 = each(mm, n1, n2)
    tmat = each(lambda x1, x2, x3, t: mm(eye - x1 + x2 - x3, t).astype(BF16), n1, n2, n3, t_d)
    return [dict(z=p["z"], xk=p["xk"], xr=p["xr"], ybg=p["ybg"], vs=p["vs"], gam=p["gam"], bonus=p["bonus"],
                 tmat=t, avk=k, avr=r, arb=b, ds0=d)
            for p, t, k, r, b, d in zip(pre, tmat, avk, avr, arb, ds0)]


def _rwkv_advance(pre, s_prev, lnw, lnb, masks):
    sb = [s.astype(BF16) for s in s_prev]
    rhs = [_dot_nt(p["xk"], s) + p["avk"] for p, s in zip(pre, sb)]
    osr = [_dot_nt(p["xr"], s) + p["avr"] for p, s in zip(pre, sb)]
    ub = [_dot(p["tmat"], x.astype(BF16)).astype(BF16) for p, x in zip(pre, rhs)]
    o = [x - _dot(p["arb"], u) for x, p, u in zip(osr, pre, ub)]
    s_new = [s * p["gam"] + p["ds0"] - _dot_tn(u, p["ybg"]) for s, p, u in zip(s_prev, pre, ub)]
    full = (LANES, LANES)
    own = [masks[p["z"]][1] for p in pre]
    mu = [jnp.broadcast_to(jnp.sum(x, axis=1, keepdims=True) * (1.0 / RWKV_HEAD), full) for x in o]
    cen = [jnp.where(m, x - y, 0.0) for m, x, y in zip(own, o, mu)]
    var = [jnp.sum(x * x, axis=1, keepdims=True) * (1.0 / RWKV_HEAD) for x in cen]
    rstd = [jnp.broadcast_to(lax.rsqrt(x + GN_EPS), full) for x in var]
    y = [c * r * w + jnp.where(m, b, 0.0) + p["bonus"] for c, r, w, m, b, p in zip(cen, rstd, lnw, own, lnb, pre)]
    return [x[:CHUNK] + x[CHUNK:] for x in y], s_new


def _rwkv_masks():
    i = lax.broadcasted_iota(jnp.int32, (LANES, LANES), 0)
    j = lax.broadcasted_iota(jnp.int32, (LANES, LANES), 1)
    lane_lo = lax.broadcasted_iota(jnp.int32, (CHUNK, LANES), 1) < RWKV_HEAD
    same = (i // CHUNK) == (j // CHUNK)
    eye = jnp.where(i == j, 1.0, 0.0).astype(F32)
    diag16 = (i // 16) == (j // 16)
    out = []
    for z in range(N_DIR):
        before = (j % CHUNK) > (i % CHUNK) if z == 1 else (j % CHUNK) < (i % CHUNK)
        strict = same & before
        incl = same & (before | (i == j))
        out.append((lane_lo, same, strict, incl, diag16, eye))
    return out


def _rwkv_kernel(nch, f0_ref, f1_ref, w0_ref, wup_ref, a0_ref, aup_ref, kk_ref, ka_ref, rk_ref,
                 lnw_ref, lnb_ref, gup_ref, sin_ref, y0_ref, y1_ref, sout_ref,
                 s_ref, logw_s, kraw_s, a_s, keff_s, v_s, r_s, g_s):
    step = pl.program_id(0)

    @pl.when(step == 0)
    def _():
        s_ref[...] = sin_ref[...]

    dst = (logw_s, kraw_s, a_s, keff_s, v_s, r_s, g_s)
    for z, f_ref in ((0, f0_ref), (1, f1_ref)):
        _rwkv_prep(z, f_ref, w0_ref, wup_ref, a0_ref, aup_ref, kk_ref, ka_ref, gup_ref, dst)

    masks = _rwkv_masks()
    y_refs = (y0_ref, y1_ref)

    units, where = [], []
    for ci in range(nch):
        for z in range(N_DIR):
            r0 = (nch - 1 - ci if z == 1 else ci) * CHUNK
            rows = slice(r0, r0 + CHUNK)
            for p in range(RWKV_PAIRS):
                ls = slice(p * LANES, (p + 1) * LANES)
                units.append((z, logw_s[z, rows, ls], kraw_s[z, rows, ls], a_s[z, rows, ls], keff_s[z, rows, ls],
                              v_s[z, rows, ls], r_s[z, rows, ls], rk_ref[:, ls]))
                where.append((z, p, rows, ls))
    pre = _rwkv_setup(units, masks)
    per = N_DIR * RWKV_PAIRS
    state = [s_ref[z, p] for z in range(N_DIR) for p in range(RWKV_PAIRS)]
    lnw = [lnw_ref[:, p * LANES:(p + 1) * LANES] for z in range(N_DIR) for p in range(RWKV_PAIRS)]
    lnb = [lnb_ref[:, p * LANES:(p + 1) * LANES] for z in range(N_DIR) for p in range(RWKV_PAIRS)]
    for ci in range(nch):
        ys, state = _rwkv_advance(pre[ci * per:(ci + 1) * per], state, lnw, lnb, masks)
        for (z, p, rows, ls), y in zip(where[ci * per:(ci + 1) * per], ys):
            y_refs[z][rows, ls] = (y * g_s[z, rows, ls]).astype(BF16)
    for i, s_new in enumerate(state):
        s_ref[i // RWKV_PAIRS, i % RWKV_PAIRS] = s_new

    @pl.when(step == pl.num_programs(0) - 1)
    def _():
        sout_ref[...] = s_ref[...]


def _rwkv(feat, prm, s_init, tb):
    t = feat.shape[0]
    nb = t // tb
    nch = tb // CHUNK
    fix2 = lambda i: (0, 0)
    fix3 = lambda i: (0, 0, 0)
    fix4 = lambda i: (0, 0, 0, 0)
    fwd = lambda i: (i, 0)
    bwd = lambda i: (nb - 1 - i, 0)
    vec = pl.BlockSpec((1, D_RWKV), fix2)
    state = pl.BlockSpec((N_DIR, RWKV_PAIRS, LANES, LANES), fix4)
    big = pltpu.VMEM((N_DIR, tb, D_RWKV), F32)
    return pl.pallas_call(
        functools.partial(_rwkv_kernel, nch),
        grid=(nb,),
        in_specs=[pl.BlockSpec((tb, RWKV_COLS), fwd), pl.BlockSpec((tb, RWKV_COLS), bwd),
                  pl.BlockSpec((N_DIR, D_RWKV), fix2), pl.BlockSpec((N_DIR, LORA_W, D_RWKV), fix3),
                  pl.BlockSpec((N_DIR, D_RWKV), fix2), pl.BlockSpec((N_DIR, LORA_A, D_RWKV), fix3),
                  vec, vec, vec, vec, vec,
                  pl.BlockSpec((LORA_G, D_RWKV), fix2), state],
        out_specs=[pl.BlockSpec((tb, D_RWKV), fwd), pl.BlockSpec((tb, D_RWKV), bwd), state],
        out_shape=[jax.ShapeDtypeStruct((t, D_RWKV), BF16), jax.ShapeDtypeStruct((t, D_RWKV), BF16),
                   jax.ShapeDtypeStruct((N_DIR, RWKV_PAIRS, LANES, LANES), F32)],
        scratch_shapes=[pltpu.VMEM((N_DIR, RWKV_PAIRS, LANES, LANES), F32)] + [big] * 7,
        compiler_params=_params(("arbitrary",)),
        name="rwkv",
    )(feat, feat, prm["w0"], prm["w_up"], prm["a0"], prm["a_up"], prm["k_k"], prm["k_a"], prm["r_k"],
      prm["ln_w"], prm["ln_b"], prm["g_up"], s_init)


def _cummax_rows(x, reverse):
    n = x.shape[0]
    row = lax.broadcasted_iota(jnp.int32, x.shape, 0)
    d = 1
    while d < n:
        if reverse:
            x = jnp.maximum(x, jnp.where(row < n - d, pltpu.roll(x, n - d, axis=0), -jnp.inf))
        else:
            x = jnp.maximum(x, jnp.where(row >= d, pltpu.roll(x, d, axis=0), -jnp.inf))
        d *= 2
    return x


def _mlstm_kernel(nch, qk0_ref, qk1_ref, mv0_ref, mv1_ref, if0_ref, if1_ref, conv_ref, bias_ref, ng_ref,
                  cin_ref, min_ref, h0_ref, h1_ref, cout_ref, mout_ref,
                  c_s, m_s, qk_s, gi_s, lf_s):
    step = pl.program_id(0)
    rows_n = qk0_ref.shape[0]
    nh = MLSTM_HEADS

    @pl.when(step == 0)
    def _():
        c_s[...] = cin_ref[...]
        m_s[...] = min_ref[...]

    for z, (qk_ref, if_ref) in enumerate(((qk0_ref, if0_ref), (qk1_ref, if1_ref))):
        qk = _conv3_rows(qk_ref[...], conv_ref[...], rows_n)
        qk_s[z] = qk * _sigmoid(qk)
        gate = if_ref[...] + bias_ref[...]
        gi_s[z] = gate[:, :LANES]
        fg = gate[:, LANES:]
        lf_s[z] = jnp.minimum(fg, 0.0) - jnp.log(1.0 + jnp.exp(-jnp.abs(fg)))

    ti = lax.broadcasted_iota(jnp.int32, (CHUNK, CHUNK), 0)
    si = lax.broadcasted_iota(jnp.int32, (CHUNK, CHUNK), 1)
    causal = (si <= ti, si >= ti)
    lane = lax.broadcasted_iota(jnp.int32, (1, LANES), 1)
    mv_refs = (mv0_ref, mv1_ref)
    h_refs = (h0_ref, h1_ref)
    ones = jnp.ones((CHUNK, LANES), BF16)
    lane0 = lax.broadcasted_iota(jnp.int32, (CHUNK, LANES), 1) == 0

    mrow = m_s[0:1, :]
    us = []
    for ci in range(nch):
        m_next = mrow
        for z in range(N_DIR):
            rev = z == 1
            r0 = (nch - 1 - ci if rev else ci) * CHUNK
            rows = slice(r0, r0 + CHUNK)
            gi = gi_s[z, rows, :]
            b = _scan_rows(lf_s[z, rows, :], rev)
            d = gi - b
            bend = b[0:1, :] if rev else b[CHUNK - 1:CHUNK, :]
            m_row = b + jnp.maximum(mrow, _cummax_rows(d, rev))
            a_int = jnp.exp(b + mrow - m_row)
            c1 = b - m_row
            g_end = bend - b + gi
            m_new = jnp.maximum(bend + mrow, jnp.max(g_end, axis=0, keepdims=True))
            w_end = jnp.exp(g_end - m_new)
            keep = jnp.exp(bend + mrow - m_new)
            unit_lanes = (lane >= z * nh) & (lane < (z + 1) * nh)
            m_next = jnp.where(unit_lanes, m_new, m_next)
            d_t = d.T
            for h in range(nh):
                u = z * nh + h
                us.append(dict(
                    ci=ci, z=z, h=h, u=u, rows=rows, c1=c1[:, u:u + 1], drow=d_t[u:u + 1, :], a_int=a_int[:, u:u + 1],
                    w_end=w_end[:, u:u + 1], keep=keep[:, u:u + 1], einv=jnp.exp(-m_row[:, u:u + 1]),
                    q=qk_s[z, rows, h * MLSTM_DQK:(h + 1) * MLSTM_DQK],
                    k=qk_s[z, rows, D_MLSTM_QK + h * MLSTM_DQK:D_MLSTM_QK + (h + 1) * MLSTM_DQK] * (MLSTM_DQK ** -0.5),
                    v=mv_refs[z][rows, h * MLSTM_DV:(h + 1) * MLSTM_DV]))
        mrow = m_next
    def col(key, width):
        return [jnp.broadcast_to(x[key], (CHUNK, width)) for x in us]

    qb = [x["q"].astype(BF16) for x in us]
    kb = [x["k"].astype(BF16) for x in us]
    qk = [_dot_nt(x, y) for x, y in zip(qb, kb)]
    web = col("w_end", MLSTM_DV)
    dc = [_dot_tn(x, jnp.concatenate([(y["v"] * w).astype(BF16), jnp.where(lane0, w, 0.0).astype(BF16)], axis=1))
          for x, y, w in zip(kb, us, web)]
    c1b = col("c1", CHUNK)
    smat = [x * jnp.exp(jnp.where(causal[y["z"]], c + y["drow"], -jnp.inf)) for x, y, c in zip(qk, us, c1b)]
    sv = [_dot(x.astype(BF16), jnp.concatenate([y["v"].astype(BF16), ones], axis=1)) for x, y in zip(smat, us)]
    keepb = col("keep", 2 * MLSTM_DV)
    ct = [c_s[u] for u in range(N_DIR * nh)]
    qc = []
    for i, x in enumerate(us):
        qc.append(_dot(qb[i], ct[x["u"]].astype(BF16)))
        ct[x["u"]] = keepb[i] * ct[x["u"]] + dc[i]
    aib = col("a_int", 2 * MLSTM_DV)
    tot = [y + a * w for y, a, w in zip(sv, aib, qc)]
    rden = [1.0 / jnp.maximum(jnp.abs(x[:, MLSTM_DV:MLSTM_DV + 1]), y["einv"]) for x, y in zip(tot, us)]
    rden = [jnp.broadcast_to(x, (CHUNK, MLSTM_DV)) for x in rden]
    hh = [x[:, :MLSTM_DV] * r for x, r in zip(tot, rden)]
    hms = [jnp.mean(x * x, axis=1, keepdims=True) for x in hh]
    rstd = [jnp.broadcast_to(lax.rsqrt(x + NORM_EPS), (CHUNK, MLSTM_DV)) for x in hms]
    for x, y, r in zip(us, hh, rstd):
        cols = slice(x["h"] * MLSTM_DV, (x["h"] + 1) * MLSTM_DV)
        h_refs[x["z"]][x["rows"], cols] = (y * r * ng_ref[:, cols]).astype(BF16)
    for u, c in enumerate(ct):
        c_s[u] = c
    m_s[...] = jnp.broadcast_to(mrow, m_s.shape)

    @pl.when(step == pl.num_programs(0) - 1)
    def _():
        cout_ref[...] = c_s[...]
        mout_ref[...] = m_s[...]


def _mlstm(qk, mv, gates, prm, state, rows, ncol):
    nch = rows // CHUNK
    nu = N_DIR * MLSTM_HEADS
    qk2 = qk.reshape(rows, ncol * 2 * D_MLSTM_QK)
    mv2 = mv.reshape(rows, ncol * D_MLSTM)
    g2 = gates.reshape(rows, ncol * 2 * LANES)
    fwd = lambda i: (0, i)
    bwd = lambda i: (0, ncol - 1 - i)
    fix2 = lambda i: (0, 0)
    fix3 = lambda i: (0, 0, 0)
    cspec = pl.BlockSpec((nu, MLSTM_DQK, 2 * MLSTM_DV), fix3)
    mspec = pl.BlockSpec((8, LANES), fix2)
    wide = lambda m: pl.BlockSpec((rows, 2 * D_MLSTM_QK), m)
    outs = pl.pallas_call(
        functools.partial(_mlstm_kernel, nch),
        grid=(ncol,),
        in_specs=[wide(fwd), wide(bwd), wide(fwd), wide(bwd),
                  pl.BlockSpec((rows, 2 * LANES), fwd), pl.BlockSpec((rows, 2 * LANES), bwd),
                  pl.BlockSpec((3, 2 * D_MLSTM_QK), fix2), pl.BlockSpec((1, 2 * LANES), fix2),
                  pl.BlockSpec((1, D_MLSTM), fix2), cspec, mspec],
        out_specs=[wide(fwd), wide(bwd), cspec, mspec],
        out_shape=[jax.ShapeDtypeStruct((rows, ncol * D_MLSTM), BF16)] * 2
        + [jax.ShapeDtypeStruct((nu, MLSTM_DQK, 2 * MLSTM_DV), F32), jax.ShapeDtypeStruct((8, LANES), F32)],
        scratch_shapes=[pltpu.VMEM((nu, MLSTM_DQK, 2 * MLSTM_DV), F32), pltpu.VMEM((8, LANES), F32),
                        pltpu.VMEM((N_DIR, rows, 2 * D_MLSTM_QK), F32),
                        pltpu.VMEM((N_DIR, rows, LANES), F32), pltpu.VMEM((N_DIR, rows, LANES), F32)],
        compiler_params=_params(("arbitrary",)),
        name="mlstm",
    )(qk2, qk2, mv2, mv2, g2, g2, prm["conv"], prm["bias"], prm["norm_g"], *state)
    h0, h1 = outs[0].reshape(rows * ncol, D_MLSTM), outs[1].reshape(rows * ncol, D_MLSTM)
    return h0, h1, tuple(outs[2:])


def _merge_kernel(x_ref, ya0_ref, ya1_ref, hb0_ref, hb1_ref, po_ref, ga_ref, gb_ref, wa_ref, wb_ref, wo_ref,
                  gt1_ref, g2_ref, sh2_ref, sc2_ref, wq_ref, x1_ref, h2_ref, q_ref):
    f32 = lambda ref: ref[...].astype(F32)
    ya = f32(ya0_ref) + f32(ya1_ref)
    yb = (f32(hb0_ref) + f32(hb1_ref)) * _sigmoid(f32(po_ref))
    merged = (_sigmoid(f32(ga_ref)) * _dot(ya.astype(BF16), wa_ref[...])
              + _sigmoid(f32(gb_ref)) * _dot(yb.astype(BF16), wb_ref[...]))
    x1 = x_ref[...] + gt1_ref[...] * _dot(merged.astype(BF16), wo_ref[...])
    x1_ref[...] = x1
    y = x1 * lax.rsqrt(jnp.mean(x1 * x1, axis=-1, keepdims=True) + NORM_EPS)
    h2 = ((y * g2_ref[...]) * (1.0 + sc2_ref[...]) + sh2_ref[...]).astype(BF16)
    h2_ref[...] = h2
    q_ref[...] = _dot(h2, wq_ref[...]).astype(q_ref.dtype)


def _merge(x, ya0, ya1, hb0, hb1, po, ga, gb, wa, wb, wo, gt1, g2, sh2, sc2, wq, tm):
    t = x.shape[0]
    row = lambda i: (i, 0)
    fix = lambda i: (0, 0)
    rs = lambda n: pl.BlockSpec((tm, n), row)
    vec = pl.BlockSpec((1, D_MODEL), fix)
    nq = wq.shape[1]
    return pl.pallas_call(
        _merge_kernel,
        grid=(t // tm,),
        in_specs=[rs(D_MODEL), rs(D_RWKV), rs(D_RWKV), rs(D_MLSTM), rs(D_MLSTM), rs(D_MLSTM), rs(D_MODEL), rs(D_MODEL),
                  pl.BlockSpec(wa.shape, fix), pl.BlockSpec(wb.shape, fix), pl.BlockSpec(wo.shape, fix),
                  vec, vec, vec, vec, pl.BlockSpec(wq.shape, fix)],
        out_specs=[rs(D_MODEL), rs(D_MODEL), rs(nq)],
        out_shape=[jax.ShapeDtypeStruct((t, D_MODEL), F32), jax.ShapeDtypeStruct((t, D_MODEL), BF16),
                   jax.ShapeDtypeStruct((t, nq), BF16)],
        compiler_params=_params(("parallel",)),
        name="merge",
    )(x, ya0, ya1, hb0, hb1, po, ga, gb, wa, wb, wo, gt1, g2, sh2, sc2, wq)


def _sort_pairs(n):
    pairs = []
    t = max(1, (n - 1).bit_length())
    p = 1 << (t - 1)
    while p > 0:
        q, r, d = 1 << (t - 1), 0, p
        while d > 0:
            pairs += [(i, i + d) for i in range(n - d) if (i & p) == r]
            d, q, r = q - p, q >> 1, p
        p >>= 1
    return pairs


def _bitonic_desc(c):
    n = len(c)
    d = n // 2
    while d > 0:
        for i in range(n):
            if (i & d) == 0:
                c[i], c[i + d] = jnp.maximum(c[i], c[i + d]), jnp.minimum(c[i], c[i + d])
        d //= 2
    return c


def _merge_top(x, y):
    n = len(x)
    return _bitonic_desc([jnp.maximum(x[i], y[n - 1 - i]) for i in range(n)])


def _top16_levels(scores):
    x = [scores[8 * i:8 * (i + 1), :] for i in range(N_KEYS // 8)]
    for i, j in _sort_pairs(len(x)):
        x[i], x[j] = jnp.maximum(x[i], x[j]), jnp.minimum(x[i], x[j])
    for shift in (4, 2, 1):
        x = _merge_top(x, [pltpu.roll(v, shift, axis=0) for v in x])
    return x


def _rank_desc(levels, x):
    def probe(conds, weights, base):
        if not conds:
            return levels[base]
        return jnp.where(conds[0], probe(conds[1:], weights[1:], base + weights[0]), probe(conds[1:], weights[1:], base))

    conds, weights, rank = [], [], jnp.zeros_like(x)
    for w in (8, 4, 2, 1):
        c = probe(conds, weights, w - 1) > x
        rank = rank + jnp.where(c, float(w), 0.0)
        conds.append(c)
        weights.append(w)
    return rank + jnp.where(levels[15] > x, 1.0, 0.0)


def _router_kernel(q_ref, keys_ref, r2_ref, e2_ref, n1_ref, e1_ref):
    qb = q_ref[...]
    s1 = _dot_nt(keys_ref[0, 0], qb[:, :KEY_DIM])
    s2 = _dot_nt(keys_ref[0, 1], qb[:, KEY_DIM:])
    a = _top16_levels(s1)
    b = _top16_levels(s2)
    k = PEER_TOPK
    nj = [k // (i + 1) for i in range(k)]
    cell = {(i, j): a[i] + b[j] for i in range(k) for j in range(nj[i])}
    ninf = jnp.full(a[0].shape, -jnp.inf, F32)
    pad = lambda lst: lst + [ninf] * (k - len(lst))
    lists = [pad([cell[(i, j)] for j in range(nj[i])]) for i in range(4)]
    lists += [pad([cell[(i, j)] for i in range(4, k) if j < nj[i]]) for j in range(3)]
    top = lists[0]
    for other in lists[1:]:
        top = _merge_top(top, other)
    tau = top[k - 1]
    mx = cell[(0, 0)]
    zsum = jnp.zeros_like(tau)
    n_rank = []
    for i in range(k):
        cnt = jnp.zeros_like(tau)
        for j in range(nj[i]):
            sel = cell[(i, j)] >= tau
            zsum = zsum + jnp.where(sel, jnp.exp(cell[(i, j)] - mx), 0.0)
            cnt = cnt + jnp.where(sel, 1.0, 0.0)
        n_rank.append(cnt)
    rz = 1.0 / zsum
    for blk in range(N_KEYS // 16):
        r2, e2 = [], []
        for half in range(2):
            rows = slice(16 * blk + 8 * half, 16 * blk + 8 * (half + 1))
            s1b, s2b = s1[rows, :], s2[rows, :]
            n1 = jnp.zeros_like(s1b)
            for i in reversed(range(k)):
                n1 = jnp.where(s1b == a[i], n_rank[i], n1)
            n1_ref[0, rows, :] = n1
            e1_ref[0, rows, :] = jnp.exp(s1b - a[0]) * rz
            r2.append(_rank_desc(b, s2b))
            e2.append(jnp.exp(s2b - b[0]))
        rows = slice(16 * blk, 16 * (blk + 1))
        r2_ref[0, rows, :] = jnp.concatenate(r2, axis=0).astype(BF16)
        e2_ref[0, rows, :] = jnp.concatenate(e2, axis=0).astype(BF16)


def _router(q, keys, tt):
    t = q.shape[0]
    spec = pl.BlockSpec((1, N_KEYS, tt), lambda i, h: (h, 0, i))
    shape = lambda dt: jax.ShapeDtypeStruct((PEER_HEADS, N_KEYS, t), dt)
    return pl.pallas_call(
        _router_kernel,
        grid=(t // tt, PEER_HEADS),
        in_specs=[pl.BlockSpec((tt, PEER_QDIM), lambda i, h: (i, h)),
                  pl.BlockSpec((1, 2, N_KEYS, KEY_DIM), lambda i, h: (h, 0, 0, 0))],
        out_specs=[spec] * 4,
        out_shape=[shape(BF16), shape(BF16), shape(F32), shape(F32)],
        compiler_params=_params(("parallel", "parallel")),
        name="router",
    )(q, keys)


EXPERT_SUB = 512
TILE16 = 16
GATE_TOKENS = 512
GATE_TILES = 4


def _experts_kernel(nsub, h2_ref, u_ref, vt_ref, r2_ref, e2_ref, n1_ref, e1_ref, x1_ref, gt2_ref, fg_ref, o_ref,
                    acc_ref, act_ref, w_ref):
    e = pl.program_id(1)
    na = EXPERT_SUB // N_KEYS
    tt = h2_ref.shape[0]

    @pl.when(e == 0)
    def _():
        acc_ref[...] = jnp.zeros_like(acc_ref)

    def first_dot(si, dst):
        act_ref[dst] = _dot_nt(u_ref[si * EXPERT_SUB:(si + 1) * EXPERT_SUB, :], h2_ref[...])

    def build(si, cur):
        for ai in range(na):
            a = si * na + ai
            rows_a = slice(ai * N_KEYS, (ai + 1) * N_KEYS)
            for tb in range(tt // GATE_TOKENS):
                tok = slice(tb * GATE_TOKENS, (tb + 1) * GATE_TOKENS)
                gate = []
                for g0 in range(0, N_KEYS // TILE16, GATE_TILES):
                    part = [jnp.zeros((TILE16, GATE_TOKENS), BF16)] * GATE_TILES
                    for h in range(PEER_HEADS):
                        n1 = jnp.broadcast_to(n1_ref[h, a:a + 1, tok], (TILE16, GATE_TOKENS)).astype(BF16)
                        e1 = jnp.broadcast_to(e1_ref[h, a:a + 1, tok], (TILE16, GATE_TOKENS)).astype(BF16)
                        for k in range(GATE_TILES):
                            rows = slice((g0 + k) * TILE16, (g0 + k + 1) * TILE16)
                            zero = jnp.zeros((TILE16, GATE_TOKENS), BF16)
                            part[k] = part[k] + jnp.where(r2_ref[h, rows, tok] < n1, e2_ref[h, rows, tok], zero) * e1
                    gate += part
                act = act_ref[cur, rows_a, tok].astype(BF16)
                gl = 0.5 * act * (1.0 + lax.erf(act * (2.0 ** -0.5)))
                w_ref[cur, rows_a, tok] = jnp.concatenate(gate, axis=0) * gl

    first_dot(0, 0)
    total = None
    for si in range(nsub):
        cur = si % 2
        if si + 1 < nsub:
            first_dot(si + 1, 1 - cur)
        build(si, cur)
        part = _dot(vt_ref[si], w_ref[cur])
        total = part if total is None else total + part
    acc_ref[...] += total

    @pl.when(e == pl.num_programs(1) - 1)
    def _():
        x2 = x1_ref[...] + gt2_ref[...] * acc_ref[...].T
        y = x2 * lax.rsqrt(jnp.mean(x2 * x2, axis=-1, keepdims=True) + NORM_EPS)
        o_ref[...] = y * fg_ref[...]


def _experts(h2, u, vt, r2, e2, n1, e1, x1, gt2, fg, tt, et):
    t = h2.shape[0]
    ne = u.shape[0]
    tok = lambda i, e: (i, 0)
    fix = lambda i, e: (0, 0)
    rt = pl.BlockSpec((PEER_HEADS, N_KEYS, tt), lambda i, e: (0, 0, i))
    rs = pl.BlockSpec((PEER_HEADS, et // N_KEYS, tt), lambda i, e: (0, e, i))
    return pl.pallas_call(
        functools.partial(_experts_kernel, et // EXPERT_SUB),
        grid=(t // tt, ne // et),
        in_specs=[pl.BlockSpec((tt, D_MODEL), tok),
                  pl.BlockSpec((et, D_MODEL), lambda i, e: (e, 0)),
                  pl.BlockSpec((et // EXPERT_SUB, D_MODEL, EXPERT_SUB), lambda i, e: (e, 0, 0)),
                  rt, rt, rs, rs,
                  pl.BlockSpec((tt, D_MODEL), tok),
                  pl.BlockSpec((1, D_MODEL), fix), pl.BlockSpec((1, D_MODEL), fix)],
        out_specs=pl.BlockSpec((tt, D_MODEL), tok),
        out_shape=jax.ShapeDtypeStruct((t, D_MODEL), F32),
        scratch_shapes=[pltpu.VMEM((D_MODEL, tt), F32), pltpu.VMEM((2, EXPERT_SUB, tt), F32),
                        pltpu.VMEM((2, EXPERT_SUB, tt), BF16)],
        compiler_params=_params(("parallel", "arbitrary")),
        name="experts",
    )(h2, u, vt, r2, e2, n1, e1, x1, gt2, fg)


def _tile(n, pref):
    return pref if n % pref == 0 else n


def kernel(x, c, ctx, c_ctx, ada_w, ada_b, norm1_g, w_in, rwkv_conv, rwkv_w0, rwkv_w_up, rwkv_a0, rwkv_a_up, rwkv_g_up, rwkv_k_k, rwkv_k_a, rwkv_r_k, rwkv_ln_w, rwkv_ln_b, mlstm_conv, mlstm_i_b, mlstm_f_b, mlstm_norm_g, w_branch_a, w_branch_b, w_out, norm2_g, peer_wq, peer_keys, peer_u, peer_v, final_g):
    assert x.shape[0] == 1 and ada_w.shape[0] == 1, "one layer, batch 1"
    t, tc = x.shape[1], ctx.shape[1]
    rows = t // GRID_W
    xs, cs = x[0], ctx[0]

    cc = jnp.zeros((8, D_MODEL), F32).at[0].set(c[0]).at[1].set(c_ctx)
    mods = _ada(cc, ada_w[0], ada_b[0][None])
    sh1, sc1, gt1, sh2, sc2, gt2 = [mods[0:1, i * D_MODEL:(i + 1) * D_MODEL] for i in range(N_MOD)]
    csh1, csc1 = mods[1:2, 0:D_MODEL], mods[1:2, D_MODEL:2 * D_MODEL]

    w = w_in[0]
    o = 0
    parts = []
    for n in (RWKV_COLS, 2 * D_MLSTM_QK, D_MLSTM, 2 * N_DIR * MLSTM_HEADS, D_MLSTM, D_MODEL, D_MODEL):
        parts.append(w[:, o:o + n])
        o += n
    ng = N_DIR * MLSTM_HEADS
    gpad = ((0, 0), (0, LANES - ng))
    parts[3] = jnp.concatenate([jnp.pad(parts[3][:, :ng], gpad), jnp.pad(parts[3][:, ng:], gpad)], axis=1)
    weights = [p.astype(BF16) for p in parts]
    out_dt = [F32, F32, BF16, F32, BF16, BF16, BF16]
    g1 = norm1_g[0][None]

    rw_prm = dict(w0=rwkv_w0[0], w_up=rwkv_w_up[0].astype(BF16), a0=rwkv_a0[0],
                  a_up=rwkv_a_up[0].astype(BF16), k_k=rwkv_k_k[0][None], k_a=rwkv_k_a[0][None], r_k=rwkv_r_k[0][None],
                  ln_w=rwkv_ln_w[0][None], ln_b=rwkv_ln_b[0][None], g_up=rwkv_g_up[0].astype(BF16))
    bias = jnp.concatenate([jnp.pad(mlstm_i_b[0].reshape(1, ng), gpad), jnp.pad(mlstm_f_b[0].reshape(1, ng), gpad)], axis=1)
    ml_prm = dict(conv=mlstm_conv[0], bias=bias, norm_g=mlstm_norm_g[0][None])

    p_rw, p_qk, p_mv, p_if, _, _, _ = _proj(cs, g1, csh1, csc1, rwkv_conv[0], tc, weights, out_dt, tc)
    s_zero = jnp.zeros((N_DIR, RWKV_PAIRS, LANES, LANES), F32)
    _, _, rw_state = _rwkv(p_rw, rw_prm, s_zero, tc)
    m_zero = (jnp.zeros((ng, MLSTM_DQK, 2 * MLSTM_DV), F32), jnp.zeros((8, LANES), F32))
    _, _, ml_state = _mlstm(p_qk, p_mv, p_if, ml_prm, m_zero, tc, 1)

    p_rw, p_qk, p_mv, p_if, p_o, p_ga, p_gb = _proj(xs, g1, sh1, sc1, rwkv_conv[0], GRID_W, weights, out_dt, _tile(t, 256))
    ya0, ya1, _ = _rwkv(p_rw, rw_prm, rw_state, _tile(t, 256))
    hb0, hb1, _ = _mlstm(p_qk, p_mv, p_if, ml_prm, ml_state, rows, GRID_W)

    x1, h2, q = _merge(xs, ya0, ya1, hb0, hb1, p_o, p_ga, p_gb, w_branch_a[0].astype(BF16), w_branch_b[0].astype(BF16),
                       w_out[0].astype(BF16), gt1, norm2_g[0][None], sh2, sc2, peer_wq[0].astype(BF16), _tile(t, 256))
    r2, e2, n1, e1 = _router(q, peer_keys[0].astype(BF16), _tile(t, 1024))
    vt = peer_v[0].astype(BF16).reshape(-1, EXPERT_SUB, D_MODEL).transpose(0, 2, 1)
    out = _experts(h2, peer_u[0].astype(BF16), vt, r2, e2, n1, e1, x1, gt2, final_g[None],
                   _tile(t, 512), 2048)
    return out[None]
```

```python
import functools

import jax
import jax.numpy as jnp
from jax import lax
from jax.experimental import pallas as pl
from jax.experimental.pallas import tpu as pltpu

F32 = jnp.float32
BF16 = jnp.bfloat16

D_MODEL = 1024
GRID_W = 64
N_MOD = 6
NORM_EPS = 1e-6

RWKV_HEAD = 64
RWKV_HEADS = 8
D_RWKV = RWKV_HEADS * RWKV_HEAD
LORA_W = 64
LORA_A = 64
LORA_G = 128
RWKV_COLS = 3 * D_RWKV + LORA_W + LORA_A + LORA_G
GN_EPS = 64e-5
RWKV_PAIRS = RWKV_HEADS // 2

MLSTM_HEADS = 4
MLSTM_DQK = 64
MLSTM_DV = 128
D_MLSTM_QK = MLSTM_HEADS * MLSTM_DQK
D_MLSTM = MLSTM_HEADS * MLSTM_DV
N_DIR = 2
CHUNK = 64

PEER_HEADS = 8
N_KEYS = 128
PEER_TOPK = 16
KEY_DIM = 128
PEER_QDIM = 2 * KEY_DIM

LANES = 128
VMEM_LIMIT = 56 * 1024 * 1024

_HI = lax.Precision.HIGHEST


def _dot(a, b, precision=None):
    return jnp.dot(a, b, preferred_element_type=F32, precision=precision)


def _dot_nt(a, b):
    return lax.dot_general(a, b, (((1,), (1,)), ((), ())), preferred_element_type=F32)


def _dot_tn(a, b):
    return lax.dot_general(a, b, (((0,), (0,)), ((), ())), preferred_element_type=F32)


def _sigmoid(x):
    return 1.0 / (1.0 + jnp.exp(-x))


def _params(sem):
    return pltpu.CompilerParams(dimension_semantics=sem, vmem_limit_bytes=VMEM_LIMIT)


def _scan_rows(x, reverse):
    n = x.shape[0]
    row = lax.broadcasted_iota(jnp.int32, x.shape, 0)
    d = 1
    while d < n:
        if reverse:
            x = x + jnp.where(row < n - d, pltpu.roll(x, n - d, axis=0), 0.0)
        else:
            x = x + jnp.where(row >= d, pltpu.roll(x, d, axis=0), 0.0)
        d *= 2
    return x


def _conv3_rows(x, w, period):
    n = x.shape[0]
    pos = lax.broadcasted_iota(jnp.int32, x.shape, 0) % period
    prev = jnp.where(pos == 0, 0.0, pltpu.roll(x, 1, axis=0))
    nxt = jnp.where(pos == period - 1, 0.0, pltpu.roll(x, n - 1, axis=0))
    return w[0:1] * prev + w[1:2] * x + w[2:3] * nxt


def _ada_kernel(c_ref, w_ref, b_ref, o_ref):
    c = c_ref[...]
    s = c * _sigmoid(c)
    o_ref[...] = _dot(s, w_ref[...], precision=_HI) + b_ref[...]


def _ada(cc, w, b):
    n = w.shape[1]
    bn = n // 4
    return pl.pallas_call(
        _ada_kernel,
        grid=(n // bn,),
        in_specs=[pl.BlockSpec((8, D_MODEL), lambda i: (0, 0)),
                  pl.BlockSpec((D_MODEL, bn), lambda i: (0, i)),
                  pl.BlockSpec((1, bn), lambda i: (0, i))],
        out_specs=pl.BlockSpec((8, bn), lambda i: (0, i)),
        out_shape=jax.ShapeDtypeStruct((8, n), F32),
        compiler_params=_params(("arbitrary",)),
        name="ada",
    )(cc, w, b)


def _proj_kernel(nw, period, x_ref, g_ref, sh_ref, sc_ref, conv_ref, *refs):
    x = x_ref[...]
    y = x * lax.rsqrt(jnp.mean(x * x, axis=-1, keepdims=True) + NORM_EPS)
    h = (y * g_ref[...]) * (1.0 + sc_ref[...]) + sh_ref[...]
    hb = h.astype(BF16)
    refs[nw][...] = _conv3_rows(_dot(hb, refs[0][...]), conv_ref[...], period)
    for w_ref, o_ref in zip(refs[1:nw], refs[nw + 1:]):
        o_ref[...] = _dot(hb, w_ref[...]).astype(o_ref.dtype)


def _proj(x, g, shift, scale, conv, period, weights, dtypes, tm):
    t = x.shape[0]
    row = lambda i: (i, 0)
    fix = lambda i: (0, 0)
    assert tm % period == 0
    in_specs = [pl.BlockSpec((tm, D_MODEL), row)] + [pl.BlockSpec((1, D_MODEL), fix)] * 3 + [pl.BlockSpec(conv.shape, fix)]
    in_specs += [pl.BlockSpec(w.shape, fix) for w in weights]
    return pl.pallas_call(
        functools.partial(_proj_kernel, len(weights), period),
        grid=(t // tm,),
        in_specs=in_specs,
        out_specs=[pl.BlockSpec((tm, w.shape[1]), row) for w in weights],
        out_shape=[jax.ShapeDtypeStruct((t, w.shape[1]), dt) for w, dt in zip(weights, dtypes)],
        compiler_params=_params(("parallel",)),
        name="proj",
    )(x, g, shift, scale, conv, *weights)


def _rwkv_prep(z, f_ref, w0_ref, wup_ref, a0_ref, aup_ref, kk_ref, ka_ref, gup_ref, dst):
    f = f_ref[...]
    r = f[:, 0:D_RWKV]
    k = f[:, D_RWKV:2 * D_RWKV]
    v = f[:, 2 * D_RWKV:3 * D_RWKV]
    o = 3 * D_RWKV
    wd = f[:, o:o + LORA_W]
    ad = f[:, o + LORA_W:o + LORA_W + LORA_A]
    gd = f[:, o + LORA_W + LORA_A:o + LORA_W + LORA_A + LORA_G]
    lw = w0_ref[z:z + 1, :] + _dot(jnp.tanh(wd).astype(BF16), wup_ref[z])
    w_log = -(jnp.maximum(-lw, 0.0) + jnp.log(1.0 + jnp.exp(-jnp.abs(lw)))) - 0.5
    a = _sigmoid(a0_ref[z:z + 1, :] + _dot(ad.astype(BF16), aup_ref[z]))
    g = _dot(_sigmoid(gd).astype(BF16), gup_ref[...])
    logw_s, kraw_s, a_s, keff_s, v_s, r_s, g_s = dst
    logw_s[z] = -jnp.exp(w_log)
    kraw_s[z] = k * kk_ref[...]
    a_s[z] = a
    keff_s[z] = k * (1.0 + (a - 1.0) * ka_ref[...])
    v_s[z] = v
    r_s[z] = r
    g_s[z] = g


def _rwkv_setup(units, masks):
    def st(x, lane_lo):
        return jnp.concatenate([jnp.where(lane_lo, x, 0.0), jnp.where(lane_lo, 0.0, x)], axis=0)

    def mm(p, q):
        return _dot(p.astype(BF16), q.astype(BF16))

    def each(f, *lists):
        return [f(*args) for args in zip(*lists)]

    def bcast(col):
        return jnp.broadcast_to(col, (LANES, LANES))

    lo = [masks[u[0]][0] for u in units]
    kr = [st(u[2], m) for u, m in zip(units, lo)]
    ssq = [jnp.sum(x * x, axis=1, keepdims=True) for x in kr]
    bon = [jnp.sum(st(u[6] * u[4] * u[7], m), axis=1, keepdims=True) for u, m in zip(units, lo)]
    inv = [bcast(1.0 / jnp.maximum(jnp.sqrt(x), 1e-12)) for x in ssq]
    bon = [bcast(x) for x in bon]
    pre = []
    for (z, lw, kraw, a, keff, v, r, rk), lane_lo, iv, bo in zip(units, lo, inv, bon):
        rev = z == 1
        cum = _scan_rows(lw, rev)
        tot = cum[0:1, :] if rev else cum[CHUNK - 1:CHUNK, :]
        e_in = jnp.exp(cum)
        e_ex = jnp.exp(cum - lw)
        e_ng = jnp.exp(-cum)
        e_rem = jnp.exp(tot - cum)
        ka = kraw * a
        vs = st(v, lane_lo)
        pre.append(dict(
            z=z, xk=(st(kraw * e_ex, lane_lo) * iv).astype(BF16), xr=st(r * e_in, lane_lo).astype(BF16),
            yk=st(keff * e_ng, lane_lo).astype(BF16), yb=(st(ka * e_ng, lane_lo) * iv).astype(BF16),
            ykg=st(keff * e_rem, lane_lo).astype(BF16), ybg=(st(ka * e_rem, lane_lo) * iv).astype(BF16),
            vs=vs, vb=vs.astype(BF16), gam=jnp.exp(tot), bonus=bo * vs))
    zs = [p["z"] for p in pre]

    m = [_dot_nt(jnp.concatenate([p["xk"], p["xr"]], axis=0), jnp.concatenate([p["yk"], p["yb"]], axis=0)) for p in pre]
    akk = [jnp.where(masks[z][2], x[:LANES, :LANES], 0.0).astype(BF16) for z, x in zip(zs, m)]
    ark = [jnp.where(masks[z][3], x[LANES:, :LANES], 0.0).astype(BF16) for z, x in zip(zs, m)]
    arb = [jnp.where(masks[z][3], x[LANES:, LANES:], 0.0).astype(BF16) for z, x in zip(zs, m)]
    a_d = [jnp.where(masks[z][2] & masks[z][4], x[:LANES, LANES:], 0.0) for z, x in zip(zs, m)]
    a_off = [jnp.where(masks[z][2] & jnp.logical_not(masks[z][4]), x[:LANES, LANES:], 0.0) for z, x in zip(zs, m)]
    eye = masks[0][5]

    a2 = each(mm, a_d, a_d)
    t_d = [eye - x for x in a_d]
    avk = [_dot(k, p["vb"]) for k, p in zip(akk, pre)]
    a4 = each(mm, a2, a2)
    t_d = each(lambda t, x: t + mm(t, x), t_d, a2)
    avr = [_dot(k, p["vb"]) for k, p in zip(ark, pre)]
    a8 = each(mm, a4, a4)
    t_d = each(lambda t, x: t + mm(t, x), t_d, a4)
    ds0 = [_dot_tn(p["vb"], p["ykg"]) for p in pre]
    t_d = each(lambda t, x: t + mm(t, x), t_d, a8)
    n1 = each(mm, t_d, a_off)
    n2 = each(mm, n1, n1)
    n3 = each(mm, n1, n2)
    tmat = each(lambda x1, x2, x3, t: mm(eye - x1 + x2 - x3, t).astype(BF16), n1, n2, n3, t_d)
    return [dict(z=p["z"], xk=p["xk"], xr=p["xr"], ybg=p["ybg"], vs=p["vs"], gam=p["gam"], bonus=p["bonus"],
                 tmat=t, avk=k, avr=r, arb=b, ds0=d)
            for p, t, k, r, b, d in zip(pre, tmat, avk, avr, arb, ds0)]


def _rwkv_advance(pre, s_prev, lnw, lnb, masks):
    sb = [s.astype(BF16) for s in s_prev]
    rhs = [_dot_nt(p["xk"], s) + p["avk"] for p, s in zip(pre, sb)]
    osr = [_dot_nt(p["xr"], s) + p["avr"] for p, s in zip(pre, sb)]
    ub = [_dot(p["tmat"], x.astype(BF16)).astype(BF16) for p, x in zip(pre, rhs)]
    o = [x - _dot(p["arb"], u) for x, p, u in zip(osr, pre, ub)]
    s_new = [s * p["gam"] + p["ds0"] - _dot_tn(u, p["ybg"]) for s, p, u in zip(s_prev, pre, ub)]
    full = (LANES, LANES)
    own = [masks[p["z"]][1] for p in pre]
    mu = [jnp.broadcast_to(jnp.sum(x, axis=1, keepdims=True) * (1.0 / RWKV_HEAD), full) for x in o]
    cen = [jnp.where(m, x - y, 0.0) for m, x, y in zip(own, o, mu)]
    var = [jnp.sum(x * x, axis=1, keepdims=True) * (1.0 / RWKV_HEAD) for x in cen]
    rstd = [jnp.broadcast_to(lax.rsqrt(x + GN_EPS), full) for x in var]
    y = [c * r * w + jnp.where(m, b, 0.0) + p["bonus"] for c, r, w, m, b, p in zip(cen, rstd, lnw, own, lnb, pre)]
    return [x[:CHUNK] + x[CHUNK:] for x in y], s_new


def _rwkv_masks():
    i = lax.broadcasted_iota(jnp.int32, (LANES, LANES), 0)
    j = lax.broadcasted_iota(jnp.int32, (LANES, LANES), 1)
    lane_lo = lax.broadcasted_iota(jnp.int32, (CHUNK, LANES), 1) < RWKV_HEAD
    same = (i // CHUNK) == (j // CHUNK)
    eye = jnp.where(i == j, 1.0, 0.0).astype(F32)
    diag16 = (i // 16) == (j // 16)
    out = []
    for z in range(N_DIR):
        before = (j % CHUNK) > (i % CHUNK) if z == 1 else (j % CHUNK) < (i % CHUNK)
        strict = same & before
        incl = same & (before | (i == j))
        out.append((lane_lo, same, strict, incl, diag16, eye))
    return out


def _rwkv_kernel(nch, f0_ref, f1_ref, w0_ref, wup_ref, a0_ref, aup_ref, kk_ref, ka_ref, rk_ref,
                 lnw_ref, lnb_ref, gup_ref, sin_ref, y0_ref, y1_ref, sout_ref,
                 s_ref, logw_s, kraw_s, a_s, keff_s, v_s, r_s, g_s):
    step = pl.program_id(0)

    @pl.when(step == 0)
    def _():
        s_ref[...] = sin_ref[...]

    dst = (logw_s, kraw_s, a_s, keff_s, v_s, r_s, g_s)
    for z, f_ref in ((0, f0_ref), (1, f1_ref)):
        _rwkv_prep(z, f_ref, w0_ref, wup_ref, a0_ref, aup_ref, kk_ref, ka_ref, gup_ref, dst)

    masks = _rwkv_masks()
    y_refs = (y0_ref, y1_ref)

    units, where = [], []
    for ci in range(nch):
        for z in range(N_DIR):
            r0 = (nch - 1 - ci if z == 1 else ci) * CHUNK
            rows = slice(r0, r0 + CHUNK)
            for p in range(RWKV_PAIRS):
                ls = slice(p * LANES, (p + 1) * LANES)
                units.append((z, logw_s[z, rows, ls], kraw_s[z, rows, ls], a_s[z, rows, ls], keff_s[z, rows, ls],
                              v_s[z, rows, ls], r_s[z, rows, ls], rk_ref[:, ls]))
                where.append((z, p, rows, ls))
    pre = _rwkv_setup(units, masks)
    per = N_DIR * RWKV_PAIRS
    state = [s_ref[z, p] for z in range(N_DIR) for p in range(RWKV_PAIRS)]
    lnw = [lnw_ref[:, p * LANES:(p + 1) * LANES] for z in range(N_DIR) for p in range(RWKV_PAIRS)]
    lnb = [lnb_ref[:, p * LANES:(p + 1) * LANES] for z in range(N_DIR) for p in range(RWKV_PAIRS)]
    for ci in range(nch):
        ys, state = _rwkv_advance(pre[ci * per:(ci + 1) * per], state, lnw, lnb, masks)
        for (z, p, rows, ls), y in zip(where[ci * per:(ci + 1) * per], ys):
            y_refs[z][rows, ls] = (y * g_s[z, rows, ls]).astype(BF16)
    for i, s_new in enumerate(state):
        s_ref[i // RWKV_PAIRS, i % RWKV_PAIRS] = s_new

    @pl.when(step == pl.num_programs(0) - 1)
    def _():
        sout_ref[...] = s_ref[...]


def _rwkv(feat, prm, s_init, tb):
    t = feat.shape[0]
    nb = t // tb
    nch = tb // CHUNK
    fix2 = lambda i: (0, 0)
    fix3 = lambda i: (0, 0, 0)
    fix4 = lambda i: (0, 0, 0, 0)
    fwd = lambda i: (i, 0)
    bwd = lambda i: (nb - 1 - i, 0)
    vec = pl.BlockSpec((1, D_RWKV), fix2)
    state = pl.BlockSpec((N_DIR, RWKV_PAIRS, LANES, LANES), fix4)
    big = pltpu.VMEM((N_DIR, tb, D_RWKV), F32)
    return pl.pallas_call(
        functools.partial(_rwkv_kernel, nch),
        grid=(nb,),
        in_specs=[pl.BlockSpec((tb, RWKV_COLS), fwd), pl.BlockSpec((tb, RWKV_COLS), bwd),
                  pl.BlockSpec((N_DIR, D_RWKV), fix2), pl.BlockSpec((N_DIR, LORA_W, D_RWKV), fix3),
                  pl.BlockSpec((N_DIR, D_RWKV), fix2), pl.BlockSpec((N_DIR, LORA_A, D_RWKV), fix3),
                  vec, vec, vec, vec, vec,
                  pl.BlockSpec((LORA_G, D_RWKV), fix2), state],
        out_specs=[pl.BlockSpec((tb, D_RWKV), fwd), pl.BlockSpec((tb, D_RWKV), bwd), state],
        out_shape=[jax.ShapeDtypeStruct((t, D_RWKV), BF16), jax.ShapeDtypeStruct((t, D_RWKV), BF16),
                   jax.ShapeDtypeStruct((N_DIR, RWKV_PAIRS, LANES, LANES), F32)],
        scratch_shapes=[pltpu.VMEM((N_DIR, RWKV_PAIRS, LANES, LANES), F32)] + [big] * 7,
        compiler_params=_params(("arbitrary",)),
        name="rwkv",
    )(feat, feat, prm["w0"], prm["w_up"], prm["a0"], prm["a_up"], prm["k_k"], prm["k_a"], prm["r_k"],
      prm["ln_w"], prm["ln_b"], prm["g_up"], s_init)


def _cummax_rows(x, reverse):
    n = x.shape[0]
    row = lax.broadcasted_iota(jnp.int32, x.shape, 0)
    d = 1
    while d < n:
        if reverse:
            x = jnp.maximum(x, jnp.where(row < n - d, pltpu.roll(x, n - d, axis=0), -jnp.inf))
        else:
            x = jnp.maximum(x, jnp.where(row >= d, pltpu.roll(x, d, axis=0), -jnp.inf))
        d *= 2
    return x


def _mlstm_kernel(nch, qk0_ref, qk1_ref, mv0_ref, mv1_ref, if0_ref, if1_ref, conv_ref, bias_ref, ng_ref,
                  cin_ref, min_ref, h0_ref, h1_ref, cout_ref, mout_ref,
                  c_s, m_s, qk_s, gi_s, lf_s):
    step = pl.program_id(0)
    rows_n = qk0_ref.shape[0]
    nh = MLSTM_HEADS

    @pl.when(step == 0)
    def _():
        c_s[...] = cin_ref[...]
        m_s[...] = min_ref[...]

    for z, (qk_ref, if_ref) in enumerate(((qk0_ref, if0_ref), (qk1_ref, if1_ref))):
        qk = _conv3_rows(qk_ref[...].astype(F32), conv_ref[...], rows_n)
        qk_s[z] = qk * _sigmoid(qk)
        gate = if_ref[...] + bias_ref[...]
        gi_s[z] = gate[:, :LANES]
        fg = gate[:, LANES:]
        lf_s[z] = jnp.minimum(fg, 0.0) - jnp.log(1.0 + jnp.exp(-jnp.abs(fg)))

    ti = lax.broadcasted_iota(jnp.int32, (CHUNK, CHUNK), 0)
    si = lax.broadcasted_iota(jnp.int32, (CHUNK, CHUNK), 1)
    causal = (si <= ti, si >= ti)
    lane = lax.broadcasted_iota(jnp.int32, (1, LANES), 1)
    mv_refs = (mv0_ref, mv1_ref)
    h_refs = (h0_ref, h1_ref)
    ones = jnp.ones((CHUNK, LANES), BF16)
    lane0 = lax.broadcasted_iota(jnp.int32, (CHUNK, LANES), 1) == 0

    mrow = m_s[0:1, :]
    us = []
    for ci in range(nch):
        m_next = mrow
        for z in range(N_DIR):
            rev = z == 1
            r0 = (nch - 1 - ci if rev else ci) * CHUNK
            rows = slice(r0, r0 + CHUNK)
            gi = gi_s[z, rows, :]
            b = _scan_rows(lf_s[z, rows, :], rev)
            d = gi - b
            bend = b[0:1, :] if rev else b[CHUNK - 1:CHUNK, :]
            m_row = b + jnp.maximum(mrow, _cummax_rows(d, rev))
            a_int = jnp.exp(b + mrow - m_row)
            c1 = b - m_row
            g_end = bend - b + gi
            m_new = jnp.maximum(bend + mrow, jnp.max(g_end, axis=0, keepdims=True))
            w_end = jnp.exp(g_end - m_new)
            keep = jnp.exp(bend + mrow - m_new)
            unit_lanes = (lane >= z * nh) & (lane < (z + 1) * nh)
            m_next = jnp.where(unit_lanes, m_new, m_next)
            d_t = d.T
            for h in range(nh):
                u = z * nh + h
                us.append(dict(
                    ci=ci, z=z, h=h, u=u, rows=rows, c1=c1[:, u:u + 1], drow=d_t[u:u + 1, :], a_int=a_int[:, u:u + 1],
                    w_end=w_end[:, u:u + 1], keep=keep[:, u:u + 1], einv=jnp.exp(-m_row[:, u:u + 1]),
                    q=qk_s[z, rows, h * MLSTM_DQK:(h + 1) * MLSTM_DQK],
                    k=qk_s[z, rows, D_MLSTM_QK + h * MLSTM_DQK:D_MLSTM_QK + (h + 1) * MLSTM_DQK] * (MLSTM_DQK ** -0.5),
                    v=mv_refs[z][rows, h * MLSTM_DV:(h + 1) * MLSTM_DV]))
        mrow = m_next
    def col(key, width):
        return [jnp.broadcast_to(x[key], (CHUNK, width)) for x in us]

    qb = [x["q"].astype(BF16) for x in us]
    kb = [x["k"].astype(BF16) for x in us]
    qk = [_dot_nt(x, y) for x, y in zip(qb, kb)]
    web = col("w_end", MLSTM_DV)
    dc = [_dot_tn(x, jnp.concatenate([(y["v"] * w).astype(BF16), jnp.where(lane0, w, 0.0).astype(BF16)], axis=1))
          for x, y, w in zip(kb, us, web)]
    c1b = col("c1", CHUNK)
    smat = [x * jnp.exp(jnp.where(causal[y["z"]], c + y["drow"], -jnp.inf)) for x, y, c in zip(qk, us, c1b)]
    sv = [_dot(x.astype(BF16), jnp.concatenate([y["v"].astype(BF16), ones], axis=1)) for x, y in zip(smat, us)]
    keepb = col("keep", 2 * MLSTM_DV)
    ct = [c_s[u] for u in range(N_DIR * nh)]
    qc = []
    for i, x in enumerate(us):
        qc.append(_dot(qb[i], ct[x["u"]].astype(BF16)))
        ct[x["u"]] = keepb[i] * ct[x["u"]] + dc[i]
    aib = col("a_int", 2 * MLSTM_DV)
    tot = [y + a * w for y, a, w in zip(sv, aib, qc)]
    rden = [1.0 / jnp.maximum(jnp.abs(x[:, MLSTM_DV:MLSTM_DV + 1]), y["einv"]) for x, y in zip(tot, us)]
    rden = [jnp.broadcast_to(x, (CHUNK, MLSTM_DV)) for x in rden]
    hh = [x[:, :MLSTM_DV] * r for x, r in zip(tot, rden)]
    hms = [jnp.mean(x * x, axis=1, keepdims=True) for x in hh]
    rstd = [jnp.broadcast_to(lax.rsqrt(x + NORM_EPS), (CHUNK, MLSTM_DV)) for x in hms]
    for x, y, r in zip(us, hh, rstd):
        cols = slice(x["h"] * MLSTM_DV, (x["h"] + 1) * MLSTM_DV)
        h_refs[x["z"]][x["rows"], cols] = (y * r * ng_ref[:, cols]).astype(BF16)
    for u, c in enumerate(ct):
        c_s[u] = c
    m_s[...] = jnp.broadcast_to(mrow, m_s.shape)

    @pl.when(step == pl.num_programs(0) - 1)
    def _():
        cout_ref[...] = c_s[...]
        mout_ref[...] = m_s[...]


def _mlstm(qk, mv, gates, prm, state, rows, ncol):
    nch = rows // CHUNK
    nu = N_DIR * MLSTM_HEADS
    qk2 = qk.reshape(rows, ncol * 2 * D_MLSTM_QK)
    mv2 = mv.reshape(rows, ncol * D_MLSTM)
    g2 = gates.reshape(rows, ncol * 2 * LANES)
    fwd = lambda i: (0, i)
    bwd = lambda i: (0, ncol - 1 - i)
    fix2 = lambda i: (0, 0)
    fix3 = lambda i: (0, 0, 0)
    cspec = pl.BlockSpec((nu, MLSTM_DQK, 2 * MLSTM_DV), fix3)
    mspec = pl.BlockSpec((8, LANES), fix2)
    wide = lambda m: pl.BlockSpec((rows, 2 * D_MLSTM_QK), m)
    outs = pl.pallas_call(
        functools.partial(_mlstm_kernel, nch),
        grid=(ncol,),
        in_specs=[wide(fwd), wide(bwd), wide(fwd), wide(bwd),
                  pl.BlockSpec((rows, 2 * LANES), fwd), pl.BlockSpec((rows, 2 * LANES), bwd),
                  pl.BlockSpec((3, 2 * D_MLSTM_QK), fix2), pl.BlockSpec((1, 2 * LANES), fix2),
                  pl.BlockSpec((1, D_MLSTM), fix2), cspec, mspec],
        out_specs=[wide(fwd), wide(bwd), cspec, mspec],
        out_shape=[jax.ShapeDtypeStruct((rows, ncol * D_MLSTM), BF16)] * 2
        + [jax.ShapeDtypeStruct((nu, MLSTM_DQK, 2 * MLSTM_DV), F32), jax.ShapeDtypeStruct((8, LANES), F32)],
        scratch_shapes=[pltpu.VMEM((nu, MLSTM_DQK, 2 * MLSTM_DV), F32), pltpu.VMEM((8, LANES), F32),
                        pltpu.VMEM((N_DIR, rows, 2 * D_MLSTM_QK), F32),
                        pltpu.VMEM((N_DIR, rows, LANES), F32), pltpu.VMEM((N_DIR, rows, LANES), F32)],
        compiler_params=_params(("arbitrary",)),
        name="mlstm",
    )(qk2, qk2, mv2, mv2, g2, g2, prm["conv"], prm["bias"], prm["norm_g"], *state)
    h0, h1 = outs[0].reshape(rows * ncol, D_MLSTM), outs[1].reshape(rows * ncol, D_MLSTM)
    return h0, h1, tuple(outs[2:])


def _merge_kernel(x_ref, ya0_ref, ya1_ref, hb0_ref, hb1_ref, po_ref, ga_ref, gb_ref, wa_ref, wb_ref, wo_ref,
                  gt1_ref, g2_ref, sh2_ref, sc2_ref, wq_ref, x1_ref, h2_ref, q_ref):
    f32 = lambda ref: ref[...].astype(F32)
    ya = f32(ya0_ref) + f32(ya1_ref)
    yb = (f32(hb0_ref) + f32(hb1_ref)) * _sigmoid(f32(po_ref))
    merged = (_sigmoid(f32(ga_ref)) * _dot(ya.astype(BF16), wa_ref[...])
              + _sigmoid(f32(gb_ref)) * _dot(yb.astype(BF16), wb_ref[...]))
    x1 = x_ref[...] + gt1_ref[...] * _dot(merged.astype(BF16), wo_ref[...])
    x1_ref[...] = x1
    y = x1 * lax.rsqrt(jnp.mean(x1 * x1, axis=-1, keepdims=True) + NORM_EPS)
    h2 = ((y * g2_ref[...]) * (1.0 + sc2_ref[...]) + sh2_ref[...]).astype(BF16)
    h2_ref[...] = h2
    q_ref[...] = _dot(h2, wq_ref[...]).astype(q_ref.dtype)


def _merge(x, ya0, ya1, hb0, hb1, po, ga, gb, wa, wb, wo, gt1, g2, sh2, sc2, wq, tm):
    t = x.shape[0]
    row = lambda i: (i, 0)
    fix = lambda i: (0, 0)
    rs = lambda n: pl.BlockSpec((tm, n), row)
    vec = pl.BlockSpec((1, D_MODEL), fix)
    nq = wq.shape[1]
    return pl.pallas_call(
        _merge_kernel,
        grid=(t // tm,),
        in_specs=[rs(D_MODEL), rs(D_RWKV), rs(D_RWKV), rs(D_MLSTM), rs(D_MLSTM), rs(D_MLSTM), rs(D_MODEL), rs(D_MODEL),
                  pl.BlockSpec(wa.shape, fix), pl.BlockSpec(wb.shape, fix), pl.BlockSpec(wo.shape, fix),
                  vec, vec, vec, vec, pl.BlockSpec(wq.shape, fix)],
        out_specs=[rs(D_MODEL), rs(D_MODEL), rs(nq)],
        out_shape=[jax.ShapeDtypeStruct((t, D_MODEL), F32), jax.ShapeDtypeStruct((t, D_MODEL), BF16),
                   jax.ShapeDtypeStruct((t, nq), BF16)],
        compiler_params=_params(("parallel",)),
        name="merge",
    )(x, ya0, ya1, hb0, hb1, po, ga, gb, wa, wb, wo, gt1, g2, sh2, sc2, wq)


def _sort_pairs(n):
    pairs = []
    t = max(1, (n - 1).bit_length())
    p = 1 << (t - 1)
    while p > 0:
        q, r, d = 1 << (t - 1), 0, p
        while d > 0:
            pairs += [(i, i + d) for i in range(n - d) if (i & p) == r]
            d, q, r = q - p, q >> 1, p
        p >>= 1
    return pairs


def _bitonic_desc(c):
    n = len(c)
    d = n // 2
    while d > 0:
        for i in range(n):
            if (i & d) == 0:
                c[i], c[i + d] = jnp.maximum(c[i], c[i + d]), jnp.minimum(c[i], c[i + d])
        d //= 2
    return c


def _merge_top(x, y):
    n = len(x)
    return _bitonic_desc([jnp.maximum(x[i], y[n - 1 - i]) for i in range(n)])


def _top16_levels(scores):
    x = [scores[8 * i:8 * (i + 1), :] for i in range(N_KEYS // 8)]
    for i, j in _sort_pairs(len(x)):
        x[i], x[j] = jnp.maximum(x[i], x[j]), jnp.minimum(x[i], x[j])
    for shift in (4, 2, 1):
        x = _merge_top(x, [pltpu.roll(v, shift, axis=0) for v in x])
    return x


def _rank_desc(levels, x):
    def probe(conds, weights, base):
        if not conds:
            return levels[base]
        return jnp.where(conds[0], probe(conds[1:], weights[1:], base + weights[0]), probe(conds[1:], weights[1:], base))

    conds, weights, rank = [], [], jnp.zeros_like(x)
    for w in (8, 4, 2, 1):
        c = probe(conds, weights, w - 1) > x
        rank = rank + jnp.where(c, float(w), 0.0)
        conds.append(c)
        weights.append(w)
    return rank + jnp.where(levels[15] > x, 1.0, 0.0)


def _router_kernel(q_ref, keys_ref, r2_ref, e2_ref, n1_ref, e1_ref):
    qb = q_ref[...]
    s1 = _dot_nt(keys_ref[0, 0], qb[:, :KEY_DIM])
    s2 = _dot_nt(keys_ref[0, 1], qb[:, KEY_DIM:])
    a = _top16_levels(s1)
    b = _top16_levels(s2)
    k = PEER_TOPK
    nj = [k // (i + 1) for i in range(k)]
    cell = {(i, j): a[i] + b[j] for i in range(k) for j in range(nj[i])}
    ninf = jnp.full(a[0].shape, -jnp.inf, F32)
    pad = lambda lst: lst + [ninf] * (k - len(lst))
    lists = [pad([cell[(i, j)] for j in range(nj[i])]) for i in range(4)]
    lists += [pad([cell[(i, j)] for i in range(4, k) if j < nj[i]]) for j in range(3)]
    top = lists[0]
    for other in lists[1:]:
        top = _merge_top(top, other)
    tau = top[k - 1]
    mx = cell[(0, 0)]
    zsum = jnp.zeros_like(tau)
    n_rank = []
    for i in range(k):
        cnt = jnp.zeros_like(tau)
        for j in range(nj[i]):
            sel = cell[(i, j)] >= tau
            zsum = zsum + jnp.where(sel, jnp.exp(cell[(i, j)] - mx), 0.0)
            cnt = cnt + jnp.where(sel, 1.0, 0.0)
        n_rank.append(cnt)
    rz = 1.0 / zsum
    for blk in range(N_KEYS // 16):
        r2, e2 = [], []
        for half in range(2):
            rows = slice(16 * blk + 8 * half, 16 * blk + 8 * (half + 1))
            s1b, s2b = s1[rows, :], s2[rows, :]
            n1 = jnp.zeros_like(s1b)
            for i in reversed(range(k)):
                n1 = jnp.where(s1b == a[i], n_rank[i], n1)
            n1_ref[0, rows, :] = n1
            e1_ref[0, rows, :] = jnp.exp(s1b - a[0]) * rz
            r2.append(_rank_desc(b, s2b))
            e2.append(jnp.exp(s2b - b[0]))
        rows = slice(16 * blk, 16 * (blk + 1))
        r2_ref[0, rows, :] = jnp.concatenate(r2, axis=0).astype(BF16)
        e2_ref[0, rows, :] = jnp.concatenate(e2, axis=0).astype(BF16)


def _router(q, keys, tt):
    t = q.shape[0]
    spec = pl.BlockSpec((1, N_KEYS, tt), lambda i, h: (h, 0, i))
    shape = lambda dt: jax.ShapeDtypeStruct((PEER_HEADS, N_KEYS, t), dt)
    return pl.pallas_call(
        _router_kernel,
        grid=(t // tt, PEER_HEADS),
        in_specs=[pl.BlockSpec((tt, PEER_QDIM), lambda i, h: (i, h)),
                  pl.BlockSpec((1, 2, N_KEYS, KEY_DIM), lambda i, h: (h, 0, 0, 0))],
        out_specs=[spec] * 4,
        out_shape=[shape(BF16), shape(BF16), shape(F32), shape(F32)],
        compiler_params=_params(("parallel", "parallel")),
        name="router",
    )(q, keys)


EXPERT_SUB = 512
TILE16 = 16
GATE_TOKENS = 512
GATE_TILES = 4


def _experts_kernel(nsub, h2_ref, u_ref, vt_ref, r2_ref, e2_ref, n1_ref, e1_ref, x1_ref, gt2_ref, fg_ref, o_ref,
                    acc_ref, act_ref, w_ref):
    e = pl.program_id(1)
    na = EXPERT_SUB // N_KEYS
    tt = h2_ref.shape[0]

    @pl.when(e == 0)
    def _():
        acc_ref[...] = jnp.zeros_like(acc_ref)

    def first_dot(si, dst):
        act_ref[dst] = _dot_nt(u_ref[si * EXPERT_SUB:(si + 1) * EXPERT_SUB, :], h2_ref[...])

    def build(si, cur):
        for ai in range(na):
            a = si * na + ai
            rows_a = slice(ai * N_KEYS, (ai + 1) * N_KEYS)
            for tb in range(tt // GATE_TOKENS):
                tok = slice(tb * GATE_TOKENS, (tb + 1) * GATE_TOKENS)
                gate = []
                for g0 in range(0, N_KEYS // TILE16, GATE_TILES):
                    part = [jnp.zeros((TILE16, GATE_TOKENS), BF16)] * GATE_TILES
                    for h in range(PEER_HEADS):
                        n1 = jnp.broadcast_to(n1_ref[h, a:a + 1, tok], (TILE16, GATE_TOKENS)).astype(BF16)
                        e1 = jnp.broadcast_to(e1_ref[h, a:a + 1, tok], (TILE16, GATE_TOKENS)).astype(BF16)
                        for k in range(GATE_TILES):
                            rows = slice((g0 + k) * TILE16, (g0 + k + 1) * TILE16)
                            zero = jnp.zeros((TILE16, GATE_TOKENS), BF16)
                            part[k] = part[k] + jnp.where(r2_ref[h, rows, tok] < n1, e2_ref[h, rows, tok], zero) * e1
                    gate += part
                act = act_ref[cur, rows_a, tok].astype(BF16)
                gl = 0.5 * act * (1.0 + lax.erf(act * (2.0 ** -0.5)))
                w_ref[cur, rows_a, tok] = jnp.concatenate(gate, axis=0) * gl

    first_dot(0, 0)
    total = None
    for si in range(nsub):
        cur = si % 2
        if si + 1 < nsub:
            first_dot(si + 1, 1 - cur)
        build(si, cur)
        part = _dot(vt_ref[si], w_ref[cur])
        total = part if total is None else total + part
    acc_ref[...] += total

    @pl.when(e == pl.num_programs(1) - 1)
    def _():
        x2 = x1_ref[...] + gt2_ref[...] * acc_ref[...].T
        y = x2 * lax.rsqrt(jnp.mean(x2 * x2, axis=-1, keepdims=True) + NORM_EPS)
        o_ref[...] = y * fg_ref[...]


def _experts(h2, u, vt, r2, e2, n1, e1, x1, gt2, fg, tt, et):
    t = h2.shape[0]
    ne = u.shape[0]
    tok = lambda i, e: (i, 0)
    fix = lambda i, e: (0, 0)
    rt = pl.BlockSpec((PEER_HEADS, N_KEYS, tt), lambda i, e: (0, 0, i))
    rs = pl.BlockSpec((PEER_HEADS, et // N_KEYS, tt), lambda i, e: (0, e, i))
    return pl.pallas_call(
        functools.partial(_experts_kernel, et // EXPERT_SUB),
        grid=(t // tt, ne // et),
        in_specs=[pl.BlockSpec((tt, D_MODEL), tok),
                  pl.BlockSpec((et, D_MODEL), lambda i, e: (e, 0)),
                  pl.BlockSpec((et // EXPERT_SUB, D_MODEL, EXPERT_SUB), lambda i, e: (e, 0, 0)),
                  rt, rt, rs, rs,
                  pl.BlockSpec((tt, D_MODEL), tok),
                  pl.BlockSpec((1, D_MODEL), fix), pl.BlockSpec((1, D_MODEL), fix)],
        out_specs=pl.BlockSpec((tt, D_MODEL), tok),
        out_shape=jax.ShapeDtypeStruct((t, D_MODEL), F32),
        scratch_shapes=[pltpu.VMEM((D_MODEL, tt), F32), pltpu.VMEM((2, EXPERT_SUB, tt), F32),
                        pltpu.VMEM((2, EXPERT_SUB, tt), BF16)],
        compiler_params=_params(("parallel", "arbitrary")),
        name="experts",
    )(h2, u, vt, r2, e2, n1, e1, x1, gt2, fg)


def _tile(n, pref):
    return pref if n % pref == 0 else n


def kernel(x, c, ctx, c_ctx, ada_w, ada_b, norm1_g, w_in, rwkv_conv, rwkv_w0, rwkv_w_up, rwkv_a0, rwkv_a_up, rwkv_g_up, rwkv_k_k, rwkv_k_a, rwkv_r_k, rwkv_ln_w, rwkv_ln_b, mlstm_conv, mlstm_i_b, mlstm_f_b, mlstm_norm_g, w_branch_a, w_branch_b, w_out, norm2_g, peer_wq, peer_keys, peer_u, peer_v, final_g):
    assert x.shape[0] == 1 and ada_w.shape[0] == 1, "one layer, batch 1"
    t, tc = x.shape[1], ctx.shape[1]
    rows = t // GRID_W
    xs, cs = x[0], ctx[0]

    cc = jnp.zeros((8, D_MODEL), F32).at[0].set(c[0]).at[1].set(c_ctx)
    mods = _ada(cc, ada_w[0], ada_b[0][None])
    sh1, sc1, gt1, sh2, sc2, gt2 = [mods[0:1, i * D_MODEL:(i + 1) * D_MODEL] for i in range(N_MOD)]
    csh1, csc1 = mods[1:2, 0:D_MODEL], mods[1:2, D_MODEL:2 * D_MODEL]

    w = w_in[0]
    o = 0
    parts = []
    for n in (RWKV_COLS, 2 * D_MLSTM_QK, D_MLSTM, 2 * N_DIR * MLSTM_HEADS, D_MLSTM, D_MODEL, D_MODEL):
        parts.append(w[:, o:o + n])
        o += n
    ng = N_DIR * MLSTM_HEADS
    gpad = ((0, 0), (0, LANES - ng))
    parts[3] = jnp.concatenate([jnp.pad(parts[3][:, :ng], gpad), jnp.pad(parts[3][:, ng:], gpad)], axis=1)
    weights = [p.astype(BF16) for p in parts]
    out_dt = [F32, BF16, BF16, F32, BF16, BF16, BF16]
    g1 = norm1_g[0][None]

    rw_prm = dict(w0=rwkv_w0[0], w_up=rwkv_w_up[0].astype(BF16), a0=rwkv_a0[0],
                  a_up=rwkv_a_up[0].astype(BF16), k_k=rwkv_k_k[0][None], k_a=rwkv_k_a[0][None], r_k=rwkv_r_k[0][None],
                  ln_w=rwkv_ln_w[0][None], ln_b=rwkv_ln_b[0][None], g_up=rwkv_g_up[0].astype(BF16))
    bias = jnp.concatenate([jnp.pad(mlstm_i_b[0].reshape(1, ng), gpad), jnp.pad(mlstm_f_b[0].reshape(1, ng), gpad)], axis=1)
    ml_prm = dict(conv=mlstm_conv[0], bias=bias, norm_g=mlstm_norm_g[0][None])

    p_rw, p_qk, p_mv, p_if, _, _, _ = _proj(cs, g1, csh1, csc1, rwkv_conv[0], tc, weights, out_dt, tc)
    s_zero = jnp.zeros((N_DIR, RWKV_PAIRS, LANES, LANES), F32)
    _, _, rw_state = _rwkv(p_rw, rw_prm, s_zero, tc)
    m_zero = (jnp.zeros((ng, MLSTM_DQK, 2 * MLSTM_DV), F32), jnp.zeros((8, LANES), F32))
    _, _, ml_state = _mlstm(p_qk, p_mv, p_if, ml_prm, m_zero, tc, 1)

    p_rw, p_qk, p_mv, p_if, p_o, p_ga, p_gb = _proj(xs, g1, sh1, sc1, rwkv_conv[0], GRID_W, weights, out_dt, _tile(t, 256))
    ya0, ya1, _ = _rwkv(p_rw, rw_prm, rw_state, _tile(t, 256))
    hb0, hb1, _ = _mlstm(p_qk, p_mv, p_if, ml_prm, ml_state, rows, GRID_W)

    x1, h2, q = _merge(xs, ya0, ya1, hb0, hb1, p_o, p_ga, p_gb, w_branch_a[0].astype(BF16), w_branch_b[0].astype(BF16),
                       w_out[0].astype(BF16), gt1, norm2_g[0][None], sh2, sc2, peer_wq[0].astype(BF16), _tile(t, 256))
    r2, e2, n1, e1 = _router(q, peer_keys[0].astype(BF16), _tile(t, 1024))
    vt = peer_v[0].astype(BF16).reshape(-1, EXPERT_SUB, D_MODEL).transpose(0, 2, 1)
    out = _experts(h2, peer_u[0].astype(BF16), vt, r2, e2, n1, e1, x1, gt2, final_g[None],
                   _tile(t, 512), 2048)
    return out[None]
```
